```python
import math
import jax
import jax.numpy as jnp
from jax import lax
import numpy as np

D_MODEL = 1024
BATCH = 16
SEQ = 2048
DEPTH = 2

N_MIXERS = 4
GROUP_W = D_MODEL // N_MIXERS
HEAD_V = 64
N_HEADS = GROUP_W // HEAD_V
D_MIX = N_MIXERS * GROUP_W
GDN_CONV = 4
RWKV_DECAY_RANK = 64
RWKV_A_RANK = 64
RWKV_GN_EPS = 64e-5
SC_CONV = 3
GLA_HEAD_K = HEAD_V // 2
GLA_K = N_HEADS * GLA_HEAD_K
GLA_RANK = 16
GLA_TAU = 16.0
CHUNK = 64
EPS = 1e-6

GDN_COLS = 4 * GROUP_W + 2 * N_HEADS
RWKV_COLS = 4 * GROUP_W + RWKV_DECAY_RANK + RWKV_A_RANK
SC_COLS = 4 * GROUP_W
GLA_COLS = 2 * GLA_K + 2 * GROUP_W + GLA_RANK
D_IN = GDN_COLS + RWKV_COLS + SC_COLS + GLA_COLS

kernel_name = 'hybrid_parallel_heads_gdn_rwkv7_shortconv_gla'


def _split(x, sizes):
    idx = [int(i) for i in np.cumsum(sizes)[:-1]]
    return jnp.split(x, idx, axis=-1)


def _rmsnorm(x, w):
    xf = x.astype(jnp.float32)
    y = xf * lax.rsqrt(jnp.mean(xf * xf, axis=-1, keepdims=True) + EPS)
    return (y * w).astype(x.dtype)


def _l2norm(x):
    return x * lax.rsqrt(jnp.sum(x * x, axis=-1, keepdims=True) + EPS)


def _causal_dwconv(x, w):
    kw = w.shape[0]
    t = x.shape[1]
    xp = jnp.pad(x, ((0, 0), (kw - 1, 0), (0, 0)))
    return sum(xp[:, i:i + t] * w[i] for i in range(kw))


def _token_shift(x):
    return jnp.pad(x, ((0, 0), (1, 0), (0, 0)))[:, :-1]


def _heads(x, d):
    b, t, c = x.shape
    return x.reshape(b, t, c // d, d).transpose(0, 2, 1, 3)


def _merge(x):
    b, h, t, d = x.shape
    return x.transpose(0, 2, 1, 3).reshape(b, t, h * d)


def _chunks(x):
    b, h, t = x.shape[:3]
    return x.reshape((b, h, t // CHUNK, CHUNK) + x.shape[3:])


def _gated_deltanet(p, conv_w, a_log, dt_bias, norm_w):
    dtype = p.dtype
    p = p.astype(jnp.float32)
    qkv, z, a_raw, b_raw = _split(p, [3 * GROUP_W, GROUP_W, N_HEADS, N_HEADS])
    qkv = jax.nn.silu(_causal_dwconv(qkv, conv_w))
    q, k, v = jnp.split(qkv, 3, axis=-1)
    q = _chunks(_l2norm(_heads(q, HEAD_V)) * HEAD_V ** -0.5)
    k = _chunks(_l2norm(_heads(k, HEAD_V)))
    v = _chunks(_heads(v, HEAD_V))
    g = -jnp.exp(a_log) * jax.nn.softplus(a_raw + dt_bias)
    g = jnp.cumsum(_chunks(g.transpose(0, 2, 1)), axis=-1)
    beta = _chunks(jax.nn.sigmoid(b_raw).transpose(0, 2, 1))
    incl = jnp.tril(jnp.ones((CHUNK, CHUNK), bool))
    strict = jnp.tril(jnp.ones((CHUNK, CHUNK), bool), -1)
    diff = g[..., :, None] - g[..., None, :]
    decay = jnp.where(incl, jnp.exp(jnp.where(incl, diff, 0.0)), 0.0)
    kb = k * beta[..., None]
    a_mat = jnp.where(strict, jnp.einsum('bhnid,bhnjd->bhnij', kb, k) * decay, 0.0)
    t_mat = a_mat + jnp.eye(CHUNK, dtype=a_mat.dtype)
    u = lax.linalg.triangular_solve(t_mat, v * beta[..., None], left_side=True, lower=True, unit_diagonal=True)
    w = lax.linalg.triangular_solve(t_mat, kb * jnp.exp(g)[..., None], left_side=True, lower=True, unit_diagonal=True)
    attn = jnp.einsum('bhnid,bhnjd->bhnij', q, k) * decay
    g_last = g[..., -1]
    k_end = k * jnp.exp(g_last[..., None] - g)[..., None]
    q_g = q * jnp.exp(g)[..., None]

    def step(s, inp):
        q_c, k_c, u_c, w_c, attn_c, gl_c = inp
        v_new = u_c - jnp.einsum('bhck,bhkv->bhcv', w_c, s)
        o = jnp.einsum('bhck,bhkv->bhcv', q_c, s) + jnp.einsum('bhij,bhjv->bhiv', attn_c, v_new)
        s = s * jnp.exp(gl_c)[..., None, None] + jnp.einsum('bhck,bhcv->bhkv', k_c, v_new)
        return s, o

    xs = tuple(jnp.moveaxis(t, 2, 0) for t in (q_g, k_end, u, w, attn, g_last))
    s0 = jnp.zeros(q.shape[:2] + (HEAD_V, HEAD_V), jnp.float32)
    _, o = lax.scan(step, s0, xs)
    o = jnp.moveaxis(o, 0, 2)
    o = o.reshape(o.shape[0], o.shape[1], -1, HEAD_V)
    out = _merge(_rmsnorm(o, norm_w)) * jax.nn.silu(z)
    return out.astype(dtype)


def _rwkv7(p, mu, w0, w_up, a0, a_up, k_k, k_a, r_k, ln_w, ln_b):
    dtype = p.dtype
    p = p.astype(jnp.float32)
    p = p + mu * (_token_shift(p) - p)
    r, k, v, z, w_down, a_down = _split(p, [GROUP_W] * 4 + [RWKV_DECAY_RANK, RWKV_A_RANK])
    decay = jnp.exp(-math.exp(-0.5) * jax.nn.sigmoid(w0 + jnp.tanh(w_down) @ w_up))
    a = jax.nn.sigmoid(a0 + a_down @ a_up)
    b_, t_ = p.shape[:2]

    def bthd(x):
        return x.reshape(b_, t_, N_HEADS, HEAD_V)

    kk = _l2norm(bthd(k * k_k))
    k = k * (1.0 + (a - 1.0) * k_a)

    def step(s, inp):
        r_t, w_t, k_t, kk_t, a_t, v_t = inp
        sa = jnp.einsum('bhvk,bhk->bhv', s, -kk_t)
        s = (s * w_t[:, :, None, :] + sa[..., None] * (kk_t * a_t)[:, :, None, :]
             + v_t[..., None] * k_t[:, :, None, :])
        return s, jnp.einsum('bhvk,bhk->bhv', s, r_t)

    xs = tuple(jnp.moveaxis(t, 1, 0) for t in (bthd(r), bthd(decay), bthd(k), kk, bthd(a), bthd(v)))
    s0 = jnp.zeros((b_, N_HEADS, HEAD_V, HEAD_V), jnp.float32)
    _, y = lax.scan(step, s0, xs)
    y = jnp.moveaxis(y, 0, 1)
    mean = jnp.mean(y, axis=-1, keepdims=True)
    var = jnp.mean(jnp.square(y - mean), axis=-1, keepdims=True)
    yn = ((y - mean) * lax.rsqrt(var + RWKV_GN_EPS)).reshape(b_, t_, GROUP_W) * ln_w + ln_b
    bonus = (jnp.sum(bthd(r * k * r_k), axis=-1, keepdims=True) * bthd(v)).reshape(b_, t_, GROUP_W)
    out = (yn + bonus) * jax.nn.silu(z)
    return out.astype(dtype)


def _short_conv(p, conv_w):
    bg, cg, xv, z = _split(p, [GROUP_W] * 4)
    return bg * _causal_dwconv(cg * xv, conv_w) * jax.nn.silu(z)


def _gla(p, a_up, a_bias, norm_w):
    dtype = p.dtype
    p = p.astype(jnp.float32)
    q, k, v, z, a_down = _split(p, [GLA_K, GLA_K, GROUP_W, GROUP_W, GLA_RANK])
    log_a = jax.nn.log_sigmoid(a_down @ a_up + a_bias) / GLA_TAU
    q = _chunks(_heads(q, GLA_HEAD_K)) * GLA_HEAD_K ** -0.5
    k = _chunks(_heads(k, GLA_HEAD_K))
    v = _chunks(_heads(v, HEAD_V))
    bcum = jnp.cumsum(_chunks(_heads(log_a, GLA_HEAD_K)), axis=-2)
    b_last = bcum[..., -1:, :]
    q_e = q * jnp.exp(bcum)
    k_e = k * jnp.exp(-bcum)
    k_end = k * jnp.exp(b_last - bcum)
    incl = jnp.tril(jnp.ones((CHUNK, CHUNK), bool))
    attn = jnp.where(incl, jnp.einsum('bhnik,bhnjk->bhnij', q_e, k_e), 0.0)
    intra = jnp.einsum('bhnij,bhnjv->bhniv', attn, v)

    def step(s, inp):
        qe_c, ke_c, v_c, bl_c, intra_c = inp
        o = jnp.einsum('bhck,bhkv->bhcv', qe_c, s) + intra_c
        s = s * jnp.exp(bl_c)[..., 0, :, None] + jnp.einsum('bhck,bhcv->bhkv', ke_c, v_c)
        return s, o

    xs = tuple(jnp.moveaxis(t, 2, 0) for t in (q_e, k_end, v, b_last, intra))
    s0 = jnp.zeros(q.shape[:2] + (GLA_HEAD_K, HEAD_V), jnp.float32)
    _, o = lax.scan(step, s0, xs)
    o = jnp.moveaxis(o, 0, 2)
    o = o.reshape(o.shape[0], o.shape[1], -1, HEAD_V)
    out = _merge(_rmsnorm(o, norm_w)) * jax.nn.silu(z)
    return out.astype(dtype)


def setup_inputs(seed: int = 0) -> dict:
    key = jax.random.key(seed)
    ks = jax.random.split(key, 24)
    f32 = jnp.float32
    L = DEPTH

    def nrm(k, shape, s):
        return s * jax.random.normal(k, shape, f32)

    dt = jnp.exp(jax.random.uniform(ks[5], (L, N_HEADS), f32, math.log(1e-3), math.log(1e-1)))
    return {
        'x': nrm(ks[0], (BATCH, SEQ, D_MODEL), 1.0),
        'pre_norm_w': 1.0 + nrm(ks[1], (L, D_MODEL), 0.02),
        'w_in': nrm(ks[2], (L, D_MODEL, D_IN), D_MODEL ** -0.5),
        'gdn_conv_w': nrm(ks[3], (L, GDN_CONV, 3 * GROUP_W), GDN_CONV ** -0.5),
        'gdn_a_log': jnp.log(jax.random.uniform(ks[4], (L, N_HEADS), f32, 1.0, 16.0)),
        'gdn_dt_bias': dt + jnp.log(-jnp.expm1(-dt)),
        'gdn_norm_w': 1.0 + nrm(ks[6], (L, HEAD_V), 0.02),
        'rwkv_mu': jax.random.uniform(ks[7], (L, RWKV_COLS), f32),
        'rwkv_w0': jax.random.uniform(ks[8], (L, GROUP_W), f32, -2.0, 2.0),
        'rwkv_w_up': nrm(ks[9], (L, RWKV_DECAY_RANK, GROUP_W), 0.5 * RWKV_DECAY_RANK ** -0.5),
        'rwkv_a0': nrm(ks[10], (L, GROUP_W), 0.1),
        'rwkv_a_up': nrm(ks[11], (L, RWKV_A_RANK, GROUP_W), 0.5 * RWKV_A_RANK ** -0.5),
        'rwkv_k_k': 0.85 + nrm(ks[12], (L, GROUP_W), 0.05),
        'rwkv_k_a': 1.0 + nrm(ks[13], (L, GROUP_W), 0.05),
        'rwkv_r_k': nrm(ks[14], (L, GROUP_W), 0.1),
        'rwkv_ln_w': 1.0 + nrm(ks[15], (L, GROUP_W), 0.02),
        'rwkv_ln_b': nrm(ks[16], (L, GROUP_W), 0.01),
        'sc_conv_w': nrm(ks[17], (L, SC_CONV, GROUP_W), SC_CONV ** -0.5),
        'gla_a_up': nrm(ks[18], (L, GLA_RANK, GLA_K), GLA_RANK ** -0.5),
        'gla_a_bias': 2.0 + nrm(ks[19], (L, GLA_K), 0.5),
        'gla_norm_w': 1.0 + nrm(ks[20], (L, HEAD_V), 0.02),
        'w_out': nrm(ks[21], (L, D_MIX, D_MODEL), D_MIX ** -0.5),
        'post_norm_w': 1.0 + nrm(ks[22], (L, D_MODEL), 0.02),
    }


def reference(x, pre_norm_w, w_in, gdn_conv_w, gdn_a_log, gdn_dt_bias, gdn_norm_w,
              rwkv_mu, rwkv_w0, rwkv_w_up, rwkv_a0, rwkv_a_up, rwkv_k_k, rwkv_k_a, rwkv_r_k,
              rwkv_ln_w, rwkv_ln_b, sc_conv_w, gla_a_up, gla_a_bias, gla_norm_w, w_out, post_norm_w):
    for l in range(DEPTH):
        h = _rmsnorm(x, pre_norm_w[l])
        proj = jnp.einsum('btd,de->bte', h, w_in[l])
        p_gdn, p_rwkv, p_sc, p_gla = _split(proj, [GDN_COLS, RWKV_COLS, SC_COLS, GLA_COLS])
        y_gdn = _gated_deltanet(p_gdn, gdn_conv_w[l], gdn_a_log[l], gdn_dt_bias[l], gdn_norm_w[l])
        y_rwkv = _rwkv7(p_rwkv, rwkv_mu[l], rwkv_w0[l], rwkv_w_up[l], rwkv_a0[l], rwkv_a_up[l],
                        rwkv_k_k[l], rwkv_k_a[l], rwkv_r_k[l], rwkv_ln_w[l], rwkv_ln_b[l])
        y_sc = _short_conv(p_sc, sc_conv_w[l])
        y_gla = _gla(p_gla, gla_a_up[l], gla_a_bias[l], gla_norm_w[l])
        y = jnp.concatenate([y_gdn, y_rwkv, y_sc, y_gla], axis=-1)
        out = jnp.einsum('bte,ed->btd', y, w_out[l])
        x = x + _rmsnorm(out, post_norm_w[l])
    return x
```

```python
import functools
import math

import jax
import jax.numpy as jnp
import numpy as np
from jax import lax
from jax.experimental import pallas as pl
from jax.experimental.pallas import tpu as pltpu

F32 = jnp.float32
BF16 = jnp.bfloat16

D_MODEL = 1024
GROUP_W = 256
HEAD_V = 64
N_HEADS = 4
GDN_CONV = 4
RWKV_RANK = 64
RWKV_GN_EPS = 64e-5
SC_CONV = 3
GLA_HEAD_K = 32
GLA_K = 128
GLA_RANK = 16
GLA_TAU = 16.0
CHUNK = 64
EPS = 1e-6
LANE = 128
HIST = 8

_GDN0 = 0
_GDN_COLS = 4 * GROUP_W + 2 * N_HEADS
_RWKV0 = _GDN0 + _GDN_COLS
_RWKV_COLS = 4 * GROUP_W + 2 * RWKV_RANK
_SC0 = _RWKV0 + _RWKV_COLS
_SC_COLS = 4 * GROUP_W
_GLA0 = _SC0 + _SC_COLS

P_GDN = 0
P_GDN_W = 6 * GROUP_W
P_RWKV = P_GDN + P_GDN_W
P_RWKV_W = 4 * GROUP_W + 2 * RWKV_RANK
P_SC = P_RWKV + P_RWKV_W
P_SC_W = 4 * GROUP_W
P_GLA = P_SC + P_SC_W
P_GLA_W = 2 * GLA_K + 2 * GROUP_W + LANE
P_TOTAL = P_GLA + P_GLA_W

(V_GDN_ALOG, V_GDN_DT, V_GDN_NW, V_RW_W0, V_RW_A0, V_RW_KK, V_RW_KA, V_RW_RK, V_RW_LNW, V_RW_LNB,
 V_GLA_NW, V_GLA_BIAS) = range(12)
N_VEC_ROWS = 16


def _mm(a, b):
    return jnp.dot(a.astype(BF16), b.astype(BF16), preferred_element_type=F32)


def _mm_nt(a, b):
    return lax.dot_general(a.astype(BF16), b.astype(BF16), (((1,), (1,)), ((), ())),
                           preferred_element_type=F32)


def _mm_tn(a, b):
    return lax.dot_general(a.astype(BF16), b.astype(BF16), (((0,), (0,)), ((), ())),
                           preferred_element_type=F32)


def _split(a):
    hi = a.astype(BF16)
    lo = (a - hi.astype(F32)).astype(BF16)
    return hi, lo


def _mm_x(a, b_exact):
    hi, lo = _split(a)
    return (jnp.dot(hi, b_exact, preferred_element_type=F32)
            + jnp.dot(lo, b_exact, preferred_element_type=F32))


def _xmm(a_exact, b):
    hi, lo = _split(b)
    return (jnp.dot(a_exact, hi, preferred_element_type=F32)
            + jnp.dot(a_exact, lo, preferred_element_type=F32))


def _silu(x):
    return x * jax.nn.sigmoid(x)


def _softplus(x):
    return jnp.maximum(x, 0.0) + jnp.log1p(jnp.exp(-jnp.abs(x)))


def _iota(shape, dim):
    return lax.broadcasted_iota(jnp.int32, shape, dim)


class _Masks:
    def __init__(self):
        row = _iota((CHUNK, GROUP_W), 0)
        col = _iota((CHUNK, GROUP_W), 1) & (CHUNK - 1)
        self.incl = row >= col
        self.strict = row > col
        self.eye = row == col
        r2 = _iota((GROUP_W, GROUP_W), 0)
        c2 = _iota((GROUP_W, GROUP_W), 1)
        self.bd = (r2 // HEAD_V) == (c2 // HEAD_V)
        self.ones_bd = jnp.where(self.bd, 1.0, 0.0).astype(BF16)
        r3 = _iota((GROUP_W, GLA_K), 0)
        c3 = _iota((GROUP_W, GLA_K), 1)
        self.bd_gla = (r3 // HEAD_V) == (c3 // GLA_HEAD_K)
        r4 = _iota((CHUNK, CHUNK), 0)
        c4 = _iota((CHUNK, CHUNK), 1)
        self.tril = jnp.where(r4 >= c4, 1.0, 0.0).astype(BF16)


def _bd(x, mask):
    x = x.astype(BF16)
    return jnp.where(mask, jnp.concatenate([x] * N_HEADS, axis=0), jnp.zeros((), BF16))


def _head_sum(x, m):
    return _mm_x(x, m.ones_bd)


def _cumsum_chunk(x, m):
    return _xmm(m.tril, x)


def _inverse_unit_lower(n, m):
    eye = jnp.where(m.eye, 1.0, 0.0)
    p = eye + n
    pw = _mm3(n, n, m)
    for step in range(1, 6):
        if step < 5:
            both = _mm3(jnp.concatenate([pw, p], axis=0), pw, m)
            p = p + both[CHUNK:]
            pw = both[:CHUNK]
        else:
            p = p + _mm3(p, pw, m)
    return p


def _mm3(a, b, m):
    a_hi, a_lo = _split(a)
    b_hi, b_lo = _split(b)
    bd_hi = _bd(b_hi, m.bd)
    bd_lo = _bd(b_lo, m.bd)
    return (jnp.dot(a_hi, bd_hi, preferred_element_type=F32)
            + jnp.dot(a_hi, bd_lo, preferred_element_type=F32)
            + jnp.dot(a_lo, bd_hi, preferred_element_type=F32))


def _gdn_chunk(q, k, v, g, beta, s, m):
    gc = _cumsum_chunk(g, m)
    g_row = jnp.sum(jnp.where(m.eye, gc, 0.0), axis=0, keepdims=True)
    g_last = gc[CHUNK - 1:CHUNK, :]
    decay = jnp.where(m.incl, jnp.exp(jnp.where(m.incl, gc - g_row, 0.0)), 0.0)
    kb = k * beta
    aq = _mm_nt(jnp.concatenate([kb, q], axis=0), _bd(k, m.bd))
    a_mat = jnp.where(m.strict, aq[:CHUNK] * decay, 0.0)
    attn = aq[CHUNK:] * decay
    t_inv = _inverse_unit_lower(-a_mat, m)
    e_gc = jnp.exp(gc)
    u = _mm3(t_inv, v * beta, m)
    w = _mm3(t_inv, kb * e_gc, m)
    wq_s = _mm(jnp.concatenate([w, q * e_gc], axis=0), s)
    v_new = u - wq_s[:CHUNK]
    o = wq_s[CHUNK:] + _mm(attn, _bd(v_new, m.bd))
    k_end = k * jnp.exp(g_last - gc)
    s = s * jnp.exp(g_last) + jnp.where(m.bd, _mm_tn(k_end, v_new), 0.0)
    return o, s


def _rwkv_chunk(r, k, v, kk, a, wlog, s, m):
    cw = _cumsum_chunk(wlog, m)
    cw_last = cw[CHUNK - 1:CHUNK, :]
    e_cw = jnp.exp(cw)
    e_ncw = jnp.exp(-cw)
    e_end = jnp.exp(cw_last - cw)
    b = kk * a
    lhs = jnp.concatenate([-kk * jnp.exp(cw - wlog), r * e_cw], axis=0)
    m_b = _mm_nt(lhs, _bd(b * e_ncw, m.bd))
    m_k = _mm_nt(lhs, _bd(k * e_ncw, m.bd))
    a_ab = jnp.where(m.strict, m_b[:CHUNK], 0.0)
    a_rb = jnp.where(m.incl, m_b[CHUNK:], 0.0)
    a_ak = jnp.where(m.strict, m_k[:CHUNK], 0.0)
    a_rk = jnp.where(m.incl, m_k[CHUNK:], 0.0)
    t_inv = _inverse_unit_lower(a_ab, m)
    kv = _mm(jnp.concatenate([a_ak, a_rk], axis=0), _bd(v, m.bd))
    hs = _mm_nt(lhs, s)
    u = _mm3(t_inv, hs[:CHUNK] + kv[:CHUNK], m)
    y = hs[CHUNK:] + kv[CHUNK:] + _mm(a_rb, _bd(u, m.bd))
    upd = _mm_tn(jnp.concatenate([u, v], axis=0), jnp.concatenate([b * e_end, k * e_end], axis=0))
    s = s * jnp.exp(cw_last) + jnp.where(m.bd, upd, 0.0)
    return y, s


def _gla_chunk(q, k, v, loga, s, m):
    bc = _cumsum_chunk(loga, m)
    b_last = bc[CHUNK - 1:CHUNK, :]
    qe = q * (GLA_HEAD_K ** -0.5) * jnp.exp(bc)
    ke = k * jnp.exp(-bc)
    k_end = k * jnp.exp(b_last - bc)
    attn = jnp.where(m.incl, _mm_nt(qe, _bd(ke, m.bd_gla)), 0.0)
    o = _mm_nt(qe, s) + _mm(attn, _bd(v, m.bd))
    s = s * jnp.exp(b_last) + jnp.where(m.bd_gla, _mm_tn(v, k_end), 0.0)
    return o, s


def _layer_kernel(x_ref, prew_ref, win_ref, gconv_ref, vec_ref, mu_ref, w2_ref, scw_ref, gup_ref,
                  wout_ref, postw_ref, o_ref,
                  ghist, rhist, shist, s_gdn, s_rw, s_gla, ybuf, *, tb):
    t_idx = pl.program_id(1)
    n_chunks = tb // CHUNK

    @pl.when(t_idx == 0)
    def _():
        ghist[0:HIST, :] = jnp.zeros((HIST, ghist.shape[1]), F32)
        rhist[0:HIST, :] = jnp.zeros((HIST, rhist.shape[1]), F32)
        shist[0:HIST, :] = jnp.zeros((HIST, shist.shape[1]), F32)
        s_gdn[...] = jnp.zeros_like(s_gdn)
        s_rw[...] = jnp.zeros_like(s_rw)
        s_gla[...] = jnp.zeros_like(s_gla)

    m = _Masks()

    def vec(row):
        return vec_ref[row:row + 1, :]

    x = x_ref[0]
    h = x * lax.rsqrt(jnp.mean(x * x, axis=-1, keepdims=True) + EPS) * prew_ref[...]
    h = h.astype(BF16)

    def proj(start, width):
        return jnp.dot(h, win_ref[:, start:start + width], preferred_element_type=F32)

    pg = proj(P_GDN, P_GDN_W)
    ghist[HIST:HIST + tb, :] = pg[:, :3 * GROUP_W]
    qkv = ghist[pl.ds(HIST - GDN_CONV + 1, tb), :] * gconv_ref[0:1, :]
    for i in range(1, GDN_CONV):
        qkv = qkv + ghist[pl.ds(HIST - GDN_CONV + 1 + i, tb), :] * gconv_ref[i:i + 1, :]
    ghist[0:HIST, :] = ghist[tb:tb + HIST, :]
    qkv = _silu(qkv)
    q = qkv[:, :GROUP_W]
    k = qkv[:, GROUP_W:2 * GROUP_W]
    v = qkv[:, 2 * GROUP_W:]
    q = q * lax.rsqrt(_head_sum(q * q, m) + EPS) * (HEAD_V ** -0.5)
    k = k * lax.rsqrt(_head_sum(k * k, m) + EPS)
    z = pg[:, 3 * GROUP_W:4 * GROUP_W]
    g = -jnp.exp(vec(V_GDN_ALOG)) * _softplus(pg[:, 4 * GROUP_W:5 * GROUP_W] + vec(V_GDN_DT))
    beta = jax.nn.sigmoid(pg[:, 5 * GROUP_W:])
    s = s_gdn[...]
    for c in range(n_chunks):
        sl = slice(c * CHUNK, (c + 1) * CHUNK)
        o, s = _gdn_chunk(q[sl], k[sl], v[sl], g[sl], beta[sl], s, m)
        o = o * lax.rsqrt(_head_sum(o * o, m) * (1.0 / HEAD_V) + EPS) * vec(V_GDN_NW)
        ybuf[sl, 0:GROUP_W] = o * _silu(z[sl])
    s_gdn[...] = s

    pr = proj(P_RWKV, P_RWKV_W)
    rhist[HIST:HIST + tb, :] = pr
    prev = rhist[pl.ds(HIST - 1, tb), :]
    rhist[0:HIST, :] = rhist[tb:tb + HIST, :]
    pr = pr + mu_ref[0:1, :] * (prev - pr)
    r = pr[:, :GROUP_W]
    k = pr[:, GROUP_W:2 * GROUP_W]
    v = pr[:, 2 * GROUP_W:3 * GROUP_W]
    z = pr[:, 3 * GROUP_W:4 * GROUP_W]
    low = pr[:, 4 * GROUP_W:]
    low = jnp.where(_iota(low.shape, 1) < RWKV_RANK, jnp.tanh(low), low)
    up = _mm(low, w2_ref[...])
    wlog = -math.exp(-0.5) * jax.nn.sigmoid(vec(V_RW_W0) + up[:, :GROUP_W])
    a = jax.nn.sigmoid(vec(V_RW_A0) + up[:, GROUP_W:])
    kk = k * vec(V_RW_KK)
    kk = kk * lax.rsqrt(_head_sum(kk * kk, m) + EPS)
    k = k * (1.0 + (a - 1.0) * vec(V_RW_KA))
    bonus = _head_sum(r * k * vec(V_RW_RK), m) * v
    s = s_rw[...]
    for c in range(n_chunks):
        sl = slice(c * CHUNK, (c + 1) * CHUNK)
        y, s = _rwkv_chunk(r[sl], k[sl], v[sl], kk[sl], a[sl], wlog[sl], s, m)
        mean = _head_sum(y, m) * (1.0 / HEAD_V)
        yc = y - mean
        var = _head_sum(yc * yc, m) * (1.0 / HEAD_V)
        yn = yc * lax.rsqrt(var + RWKV_GN_EPS) * vec(V_RW_LNW) + vec(V_RW_LNB)
        ybuf[sl, GROUP_W:2 * GROUP_W] = (yn + bonus[sl]) * _silu(z[sl])
    s_rw[...] = s

    ps = proj(P_SC, P_SC_W)
    shist[HIST:HIST + tb, :] = ps[:, GROUP_W:2 * GROUP_W] * ps[:, 2 * GROUP_W:3 * GROUP_W]
    cv = shist[pl.ds(HIST - SC_CONV + 1, tb), :] * scw_ref[0:1, :]
    for i in range(1, SC_CONV):
        cv = cv + shist[pl.ds(HIST - SC_CONV + 1 + i, tb), :] * scw_ref[i:i + 1, :]
    shist[0:HIST, :] = shist[tb:tb + HIST, :]
    ybuf[:, 2 * GROUP_W:3 * GROUP_W] = ps[:, :GROUP_W] * cv * _silu(ps[:, 3 * GROUP_W:])

    pa = proj(P_GLA, P_GLA_W)
    q = pa[:, :GLA_K]
    k = pa[:, GLA_K:2 * GLA_K]
    v = pa[:, 2 * GLA_K:2 * GLA_K + GROUP_W]
    z = pa[:, 2 * GLA_K + GROUP_W:2 * GLA_K + 2 * GROUP_W]
    a_dn = pa[:, 2 * GLA_K + 2 * GROUP_W:]
    pre = _mm(a_dn, gup_ref[...]) + vec_ref[V_GLA_BIAS:V_GLA_BIAS + 1, :GLA_K]
    loga = -_softplus(-pre) * (1.0 / GLA_TAU)
    s = s_gla[...]
    for c in range(n_chunks):
        sl = slice(c * CHUNK, (c + 1) * CHUNK)
        o, s = _gla_chunk(q[sl], k[sl], v[sl], loga[sl], s, m)
        o = o * lax.rsqrt(_head_sum(o * o, m) * (1.0 / HEAD_V) + EPS) * vec(V_GLA_NW)
        ybuf[sl, 3 * GROUP_W:] = o * _silu(z[sl])
    s_gla[...] = s

    out = jnp.dot(ybuf[...].astype(BF16), wout_ref[...], preferred_element_type=F32)
    out = out * lax.rsqrt(jnp.mean(out * out, axis=-1, keepdims=True) + EPS) * postw_ref[...]
    o_ref[0] = x + out


def _layer(x, prew, win, gconv, vec, mu, w2, scw, gup, wout, postw, *, tb):
    bsz, seq, _ = x.shape
    const = lambda b, t: (0, 0)
    full = lambda a: pl.BlockSpec(a.shape, const)
    return pl.pallas_call(
        functools.partial(_layer_kernel, tb=tb),
        grid=(bsz, seq // tb),
        in_specs=[pl.BlockSpec((1, tb, D_MODEL), lambda b, t: (b, t, 0))]
        + [full(a) for a in (prew, win, gconv, vec, mu, w2, scw, gup, wout, postw)],
        out_specs=pl.BlockSpec((1, tb, D_MODEL), lambda b, t: (b, t, 0)),
        out_shape=jax.ShapeDtypeStruct(x.shape, F32),
        scratch_shapes=[
            pltpu.VMEM((HIST + tb, 3 * GROUP_W), F32),
            pltpu.VMEM((HIST + tb, P_RWKV_W), F32),
            pltpu.VMEM((HIST + tb, GROUP_W), F32),
            pltpu.VMEM((GROUP_W, GROUP_W), F32),
            pltpu.VMEM((GROUP_W, GROUP_W), F32),
            pltpu.VMEM((GROUP_W, GLA_K), F32),
            pltpu.VMEM((tb, 4 * GROUP_W), F32),
        ],
        compiler_params=pltpu.CompilerParams(
            dimension_semantics=("parallel", "arbitrary"),
            vmem_limit_bytes=56 * 1024 * 1024),
        name="hybrid_layer",
    )(x, prew, win, gconv, vec, mu, w2, scw, gup, wout, postw)


def _pad_rows(a, rows):
    return jnp.pad(a, ((0, rows - a.shape[0]), (0, 0)))


def _prep_layer(w_in, gdn_conv_w, gdn_a_log, gdn_dt_bias, gdn_norm_w, rwkv_mu, rwkv_w0, rwkv_w_up,
                rwkv_a0, rwkv_a_up, rwkv_k_k, rwkv_k_a, rwkv_r_k, rwkv_ln_w, rwkv_ln_b, sc_conv_w,
                gla_a_up, gla_a_bias, gla_norm_w):
    g0 = _GDN0
    gdn = [w_in[:, g0:g0 + 4 * GROUP_W],
           jnp.repeat(w_in[:, g0 + 4 * GROUP_W:g0 + 4 * GROUP_W + N_HEADS], HEAD_V, axis=1),
           jnp.repeat(w_in[:, g0 + 4 * GROUP_W + N_HEADS:g0 + _GDN_COLS], HEAD_V, axis=1)]
    rw = [w_in[:, _RWKV0:_RWKV0 + _RWKV_COLS]]
    sc = [w_in[:, _SC0:_SC0 + _SC_COLS]]
    gla_w = 2 * GLA_K + 2 * GROUP_W
    gla = [w_in[:, _GLA0:_GLA0 + gla_w],
           jnp.pad(w_in[:, _GLA0 + gla_w:_GLA0 + gla_w + GLA_RANK], ((0, 0), (0, LANE - GLA_RANK)))]
    win = jnp.concatenate(gdn + rw + sc + gla, axis=1).astype(BF16)

    def per_head(a):
        return jnp.repeat(a, HEAD_V)

    def per_dim(a):
        return jnp.tile(a, N_HEADS)

    rows = [None] * 12
    rows[V_GDN_ALOG] = per_head(gdn_a_log)
    rows[V_GDN_DT] = per_head(gdn_dt_bias)
    rows[V_GDN_NW] = per_dim(gdn_norm_w)
    rows[V_RW_W0] = rwkv_w0
    rows[V_RW_A0] = rwkv_a0
    rows[V_RW_KK] = rwkv_k_k
    rows[V_RW_KA] = rwkv_k_a
    rows[V_RW_RK] = rwkv_r_k
    rows[V_RW_LNW] = rwkv_ln_w
    rows[V_RW_LNB] = rwkv_ln_b
    rows[V_GLA_NW] = per_dim(gla_norm_w)
    rows[V_GLA_BIAS] = jnp.pad(gla_a_bias, (0, GROUP_W - GLA_K))
    vec = _pad_rows(jnp.stack(rows), N_VEC_ROWS)

    w2 = jnp.zeros((2 * RWKV_RANK, 2 * GROUP_W), F32)
    w2 = w2.at[:RWKV_RANK, :GROUP_W].set(rwkv_w_up).at[RWKV_RANK:, GROUP_W:].set(rwkv_a_up)
    gup = _pad_rows(gla_a_up, LANE)
    return (win, _pad_rows(gdn_conv_w, HIST), vec, rwkv_mu[None, :], w2.astype(BF16),
            _pad_rows(sc_conv_w, HIST), gup.astype(BF16))


def _time_block(seq):
    for tb in (256, 128, 64):
        if seq % tb == 0:
            return tb
    raise ValueError("sequence length must be a multiple of the chunk length")


def kernel(x, pre_norm_w, w_in, gdn_conv_w, gdn_a_log, gdn_dt_bias, gdn_norm_w, rwkv_mu, rwkv_w0,
           rwkv_w_up, rwkv_a0, rwkv_a_up, rwkv_k_k, rwkv_k_a, rwkv_r_k, rwkv_ln_w, rwkv_ln_b,
           sc_conv_w, gla_a_up, gla_a_bias, gla_norm_w, w_out, post_norm_w):
    tb = _time_block(x.shape[1])
    for l in range(w_in.shape[0]):
        win, gconv, vec, mu, w2, scw, gup = _prep_layer(
            w_in[l], gdn_conv_w[l], gdn_a_log[l], gdn_dt_bias[l], gdn_norm_w[l], rwkv_mu[l],
            rwkv_w0[l], rwkv_w_up[l], rwkv_a0[l], rwkv_a_up[l], rwkv_k_k[l], rwkv_k_a[l],
            rwkv_r_k[l], rwkv_ln_w[l], rwkv_ln_b[l], sc_conv_w[l], gla_a_up[l], gla_a_bias[l],
            gla_norm_w[l])
        x = _layer(x, pre_norm_w[l][None, :], win, gconv, vec, mu, w2, scw, gup,
                   w_out[l].astype(BF16), post_norm_w[l][None, :], tb=tb)
    return x
```

```python
import functools
import math

import jax
import jax.numpy as jnp
from jax import lax
from jax.experimental import pallas as pl
from jax.experimental.pallas import tpu as pltpu

F32 = jnp.float32
BF16 = jnp.bfloat16

D_MODEL = 1024
GROUP_W = 256
HEAD_V = 64
N_HEADS = 4
GDN_CONV = 4
RWKV_RANK = 64
RWKV_GN_EPS = 64e-5
SC_CONV = 3
GLA_HEAD_K = 32
GLA_K = 128
GLA_RANK = 16
GLA_TAU = 16.0
CHUNK = 64
EPS = 1e-6
LANE = 128
HIST = 8

_GDN0 = 0
_GDN_COLS = 4 * GROUP_W + 2 * N_HEADS
_RWKV0 = _GDN0 + _GDN_COLS
_RWKV_COLS = 4 * GROUP_W + 2 * RWKV_RANK
_SC0 = _RWKV0 + _RWKV_COLS
_SC_COLS = 4 * GROUP_W
_GLA0 = _SC0 + _SC_COLS

P_GDN = 0
P_GDN_W = 6 * GROUP_W
P_RWKV = P_GDN + P_GDN_W
P_RWKV_W = 4 * GROUP_W + 2 * RWKV_RANK
P_SC = P_RWKV + P_RWKV_W
P_SC_W = 4 * GROUP_W
P_GLA = P_SC + P_SC_W
P_GLA_W = 2 * GLA_K + 2 * GROUP_W + LANE
P_TOTAL = P_GLA + P_GLA_W

(V_GDN_ALOG, V_GDN_DT, V_GDN_NW, V_RW_W0, V_RW_A0, V_RW_KK, V_RW_KA, V_RW_RK, V_RW_LNW, V_RW_LNB,
 V_GLA_NW, V_GLA_BIAS) = range(12)
N_VEC_ROWS = 16


def _mm(a, b):
    return jnp.dot(a.astype(BF16), b.astype(BF16), preferred_element_type=F32)


def _mm_nt(a, b):
    return lax.dot_general(a.astype(BF16), b.astype(BF16), (((1,), (1,)), ((), ())),
                           preferred_element_type=F32)


def _mm_tn(a, b):
    return lax.dot_general(a.astype(BF16), b.astype(BF16), (((0,), (0,)), ((), ())),
                           preferred_element_type=F32)


def _split(a):
    hi = a.astype(BF16)
    lo = (a - hi.astype(F32)).astype(BF16)
    return hi, lo


def _mm_x(a, b_exact):
    hi, lo = _split(a)
    return (jnp.dot(hi, b_exact, preferred_element_type=F32)
            + jnp.dot(lo, b_exact, preferred_element_type=F32))


def _xmm(a_exact, b):
    hi, lo = _split(b)
    return (jnp.dot(a_exact, hi, preferred_element_type=F32)
            + jnp.dot(a_exact, lo, preferred_element_type=F32))


def _silu(x):
    return x * jax.nn.sigmoid(x)


def _softplus(x):
    return jnp.maximum(x, 0.0) + jnp.log1p(jnp.exp(-jnp.abs(x)))


def _iota(shape, dim):
    return lax.broadcasted_iota(jnp.int32, shape, dim)


class _Masks:
    def __init__(self):
        row = _iota((CHUNK, GROUP_W), 0)
        col = _iota((CHUNK, GROUP_W), 1) & (CHUNK - 1)
        self.incl = row >= col
        self.strict = row > col
        self.eye = row == col
        r2 = _iota((GROUP_W, GROUP_W), 0)
        c2 = _iota((GROUP_W, GROUP_W), 1)
        self.bd = (r2 // HEAD_V) == (c2 // HEAD_V)
        self.ones_bd = jnp.where(self.bd, 1.0, 0.0).astype(BF16)
        r3 = _iota((GROUP_W, GLA_K), 0)
        c3 = _iota((GROUP_W, GLA_K), 1)
        self.bd_gla = (r3 // HEAD_V) == (c3 // GLA_HEAD_K)
        r4 = _iota((CHUNK, CHUNK), 0)
        c4 = _iota((CHUNK, CHUNK), 1)
        self.tril = jnp.where(r4 >= c4, 1.0, 0.0).astype(BF16)


def _bd(x, mask):
    x = x.astype(BF16)
    return jnp.where(mask, jnp.concatenate([x] * N_HEADS, axis=0), jnp.zeros((), BF16))


def _head_sum(x, m):
    return _mm_x(x, m.ones_bd)


def _cumsum_chunk(x, m):
    return _xmm(m.tril, x)


def _mm_heads(a, b, m):
    return jnp.dot(a.astype(BF16), _bd(b, m.bd), preferred_element_type=F32)


def _inverse_unit_lower(ns, m):
    eye = jnp.where(m.eye, 1.0, 0.0)
    ps = [eye + n for n in ns]
    pws = [_mm_heads(n, n, m) for n in ns]
    for _ in range(4):
        both = [_mm_heads(jnp.concatenate([pw, p], axis=0), pw, m)
                for pw, p in zip(pws, ps)]
        ps = [p + b[CHUNK:] for p, b in zip(ps, both)]
        pws = [b[:CHUNK] for b in both]
    return [p + _mm_heads(p, pw, m) for p, pw in zip(ps, pws)]


def _gdn_prep(q, k, v, g, beta, m):
    gc = _cumsum_chunk(g, m)
    g_row = jnp.sum(jnp.where(m.eye, gc, 0.0), axis=0, keepdims=True)
    g_last = gc[CHUNK - 1:CHUNK, :]
    decay = jnp.where(m.incl, jnp.exp(jnp.where(m.incl, gc - g_row, 0.0)), 0.0)
    kb = k * beta
    aq = _mm_nt(jnp.concatenate([kb, q], axis=0), _bd(k, m.bd))
    e_gc = jnp.exp(gc)
    return dict(n=-jnp.where(m.strict, aq[:CHUNK] * decay, 0.0), attn=aq[CHUNK:] * decay,
                vb=v * beta, kbg=kb * e_gc, qg=q * e_gc, k_end=k * jnp.exp(g_last - gc),
                e_last=jnp.exp(g_last))


def _gdn_step(c, t_inv_uw, s, m):
    u, w = t_inv_uw
    wq_s = _mm(jnp.concatenate([w, c["qg"]], axis=0), s)
    v_new = u - wq_s[:CHUNK]
    o = wq_s[CHUNK:] + _mm(c["attn"], _bd(v_new, m.bd))
    s = s * c["e_last"] + jnp.where(m.bd, _mm_tn(c["k_end"], v_new), 0.0)
    return o, s


def _rwkv_prep(r, k, v, kk, a, wlog, m):
    cw = _cumsum_chunk(wlog, m)
    cw_last = cw[CHUNK - 1:CHUNK, :]
    e_ncw = jnp.exp(-cw)
    e_end = jnp.exp(cw_last - cw)
    b = kk * a
    lhs = jnp.concatenate([-kk * jnp.exp(cw - wlog), r * jnp.exp(cw)], axis=0)
    m_b = _mm_nt(lhs, _bd(b * e_ncw, m.bd))
    m_k = _mm_nt(lhs, _bd(k * e_ncw, m.bd))
    a_ak = jnp.where(m.strict, m_k[:CHUNK], 0.0)
    a_rk = jnp.where(m.incl, m_k[CHUNK:], 0.0)
    kv = _mm(jnp.concatenate([a_ak, a_rk], axis=0), _bd(v, m.bd))
    return dict(n=jnp.where(m.strict, m_b[:CHUNK], 0.0), a_rb=jnp.where(m.incl, m_b[CHUNK:], 0.0),
                lhs=lhs, kv=kv, v=v, ends=jnp.concatenate([b * e_end, k * e_end], axis=0),
                e_last=jnp.exp(cw_last))


def _rwkv_step(c, t_inv, s, m):
    hs = _mm_nt(c["lhs"], s)
    u = _mm_heads(t_inv, hs[:CHUNK] + c["kv"][:CHUNK], m)
    y = hs[CHUNK:] + c["kv"][CHUNK:] + _mm(c["a_rb"], _bd(u, m.bd))
    upd = _mm_tn(jnp.concatenate([u, c["v"]], axis=0), c["ends"])
    s = s * c["e_last"] + jnp.where(m.bd, upd, 0.0)
    return y, s


def _gla_prep(q, k, v, loga, m):
    bc = _cumsum_chunk(loga, m)
    b_last = bc[CHUNK - 1:CHUNK, :]
    qe = q * (GLA_HEAD_K ** -0.5) * jnp.exp(bc)
    ke = k * jnp.exp(-bc)
    attn = jnp.where(m.incl, _mm_nt(qe, _bd(ke, m.bd_gla)), 0.0)
    return dict(qe=qe, intra=_mm(attn, _bd(v, m.bd)), v=v, k_end=k * jnp.exp(b_last - bc),
                e_last=jnp.exp(b_last))


def _gla_step(c, s, m):
    o = _mm_nt(c["qe"], s) + c["intra"]
    s = s * c["e_last"] + jnp.where(m.bd_gla, _mm_tn(c["v"], c["k_end"]), 0.0)
    return o, s


def _layer_kernel(x_ref, prew_ref, win_ref, gconv_ref, vec_ref, mu_ref, w2_ref, scw_ref, gup_ref,
                  wout_ref, postw_ref, o_ref,
                  ghist, rhist, shist, s_gdn, s_rw, s_gla, ybuf, *, tb):
    t_idx = pl.program_id(1)
    chunks = [slice(c * CHUNK, (c + 1) * CHUNK) for c in range(tb // CHUNK)]

    @pl.when(t_idx == 0)
    def _():
        ghist[0:HIST, :] = jnp.zeros((HIST, ghist.shape[1]), F32)
        rhist[0:HIST, :] = jnp.zeros((HIST, rhist.shape[1]), F32)
        shist[0:HIST, :] = jnp.zeros((HIST, shist.shape[1]), F32)
        s_gdn[...] = jnp.zeros_like(s_gdn)
        s_rw[...] = jnp.zeros_like(s_rw)
        s_gla[...] = jnp.zeros_like(s_gla)

    m = _Masks()

    def vec(row):
        return vec_ref[row:row + 1, :]

    x = x_ref[0]
    h = x * lax.rsqrt(jnp.mean(x * x, axis=-1, keepdims=True) + EPS) * prew_ref[...]
    h = h.astype(BF16)

    def proj(start, width):
        return jnp.dot(h, win_ref[:, start:start + width], preferred_element_type=F32)

    pg = proj(P_GDN, P_GDN_W)
    ghist[HIST:HIST + tb, :] = pg[:, :3 * GROUP_W]
    qkv = ghist[pl.ds(HIST - GDN_CONV + 1, tb), :] * gconv_ref[0:1, :]
    for i in range(1, GDN_CONV):
        qkv = qkv + ghist[pl.ds(HIST - GDN_CONV + 1 + i, tb), :] * gconv_ref[i:i + 1, :]
    ghist[0:HIST, :] = ghist[tb:tb + HIST, :]
    qkv = _silu(qkv)
    gq = qkv[:, :GROUP_W]
    gk = qkv[:, GROUP_W:2 * GROUP_W]
    gv = qkv[:, 2 * GROUP_W:]
    gq = gq * lax.rsqrt(_head_sum(gq * gq, m) + EPS) * (HEAD_V ** -0.5)
    gk = gk * lax.rsqrt(_head_sum(gk * gk, m) + EPS)
    gz = pg[:, 3 * GROUP_W:4 * GROUP_W]
    gg = -jnp.exp(vec(V_GDN_ALOG)) * _softplus(pg[:, 4 * GROUP_W:5 * GROUP_W] + vec(V_GDN_DT))
    gbeta = jax.nn.sigmoid(pg[:, 5 * GROUP_W:])

    pr = proj(P_RWKV, P_RWKV_W)
    rhist[HIST:HIST + tb, :] = pr
    prev = rhist[pl.ds(HIST - 1, tb), :]
    rhist[0:HIST, :] = rhist[tb:tb + HIST, :]
    pr = pr + mu_ref[0:1, :] * (prev - pr)
    rr = pr[:, :GROUP_W]
    rk = pr[:, GROUP_W:2 * GROUP_W]
    rv = pr[:, 2 * GROUP_W:3 * GROUP_W]
    rz = pr[:, 3 * GROUP_W:4 * GROUP_W]
    low = pr[:, 4 * GROUP_W:]
    low = jnp.where(_iota(low.shape, 1) < RWKV_RANK, jnp.tanh(low), low)
    up = _mm(low, w2_ref[...])
    wlog = -math.exp(-0.5) * jax.nn.sigmoid(vec(V_RW_W0) + up[:, :GROUP_W])
    ra = jax.nn.sigmoid(vec(V_RW_A0) + up[:, GROUP_W:])
    rkk = rk * vec(V_RW_KK)
    rkk = rkk * lax.rsqrt(_head_sum(rkk * rkk, m) + EPS)
    rk = rk * (1.0 + (ra - 1.0) * vec(V_RW_KA))
    bonus = _head_sum(rr * rk * vec(V_RW_RK), m) * rv

    ps = proj(P_SC, P_SC_W)
    shist[HIST:HIST + tb, :] = ps[:, GROUP_W:2 * GROUP_W] * ps[:, 2 * GROUP_W:3 * GROUP_W]
    cv = shist[pl.ds(HIST - SC_CONV + 1, tb), :] * scw_ref[0:1, :]
    for i in range(1, SC_CONV):
        cv = cv + shist[pl.ds(HIST - SC_CONV + 1 + i, tb), :] * scw_ref[i:i + 1, :]
    shist[0:HIST, :] = shist[tb:tb + HIST, :]
    ybuf[:, 2 * GROUP_W:3 * GROUP_W] = ps[:, :GROUP_W] * cv * _silu(ps[:, 3 * GROUP_W:])

    pa = proj(P_GLA, P_GLA_W)
    aq = pa[:, :GLA_K]
    ak = pa[:, GLA_K:2 * GLA_K]
    av = pa[:, 2 * GLA_K:2 * GLA_K + GROUP_W]
    az = pa[:, 2 * GLA_K + GROUP_W:2 * GLA_K + 2 * GROUP_W]
    a_dn = pa[:, 2 * GLA_K + 2 * GROUP_W:]
    pre = _mm(a_dn, gup_ref[...]) + vec_ref[V_GLA_BIAS:V_GLA_BIAS + 1, :GLA_K]
    loga = -_softplus(-pre) * (1.0 / GLA_TAU)

    gdn = [_gdn_prep(gq[sl], gk[sl], gv[sl], gg[sl], gbeta[sl], m) for sl in chunks]
    rw = [_rwkv_prep(rr[sl], rk[sl], rv[sl], rkk[sl], ra[sl], wlog[sl], m) for sl in chunks]
    gla = [_gla_prep(aq[sl], ak[sl], av[sl], loga[sl], m) for sl in chunks]
    t_inv = _inverse_unit_lower([c["n"] for c in gdn] + [c["n"] for c in rw], m)
    gdn_inv, rw_inv = t_inv[:len(chunks)], t_inv[len(chunks):]
    gdn_uw = [(_mm_heads(ti, c["vb"], m), _mm_heads(ti, c["kbg"], m))
              for ti, c in zip(gdn_inv, gdn)]

    sg, sr, sa = s_gdn[...], s_rw[...], s_gla[...]
    for i, sl in enumerate(chunks):
        o, sg = _gdn_step(gdn[i], gdn_uw[i], sg, m)
        ybuf[sl, 0:GROUP_W] = o
        y, sr = _rwkv_step(rw[i], rw_inv[i], sr, m)
        ybuf[sl, GROUP_W:2 * GROUP_W] = y
        o, sa = _gla_step(gla[i], sa, m)
        ybuf[sl, 3 * GROUP_W:] = o
    s_gdn[...] = sg
    s_rw[...] = sr
    s_gla[...] = sa

    o = ybuf[:, 0:GROUP_W]
    o = o * lax.rsqrt(_head_sum(o * o, m) * (1.0 / HEAD_V) + EPS) * vec(V_GDN_NW)
    ybuf[:, 0:GROUP_W] = o * _silu(gz)
    y = ybuf[:, GROUP_W:2 * GROUP_W]
    yc = y - _head_sum(y, m) * (1.0 / HEAD_V)
    var = _head_sum(yc * yc, m) * (1.0 / HEAD_V)
    yn = yc * lax.rsqrt(var + RWKV_GN_EPS) * vec(V_RW_LNW) + vec(V_RW_LNB)
    ybuf[:, GROUP_W:2 * GROUP_W] = (yn + bonus) * _silu(rz)
    o = ybuf[:, 3 * GROUP_W:]
    o = o * lax.rsqrt(_head_sum(o * o, m) * (1.0 / HEAD_V) + EPS) * vec(V_GLA_NW)
    ybuf[:, 3 * GROUP_W:] = o * _silu(az)

    out = jnp.dot(ybuf[...].astype(BF16), wout_ref[...], preferred_element_type=F32)
    out = out * lax.rsqrt(jnp.mean(out * out, axis=-1, keepdims=True) + EPS) * postw_ref[...]
    o_ref[0] = x + out


def _layer(x, prew, win, gconv, vec, mu, w2, scw, gup, wout, postw, *, tb):
    bsz, seq, _ = x.shape
    const = lambda b, t: (0, 0)
    full = lambda a: pl.BlockSpec(a.shape, const)
    return pl.pallas_call(
        functools.partial(_layer_kernel, tb=tb),
        grid=(bsz, seq // tb),
        in_specs=[pl.BlockSpec((1, tb, D_MODEL), lambda b, t: (b, t, 0))]
        + [full(a) for a in (prew, win, gconv, vec, mu, w2, scw, gup, wout, postw)],
        out_specs=pl.BlockSpec((1, tb, D_MODEL), lambda b, t: (b, t, 0)),
        out_shape=jax.ShapeDtypeStruct(x.shape, F32),
        scratch_shapes=[
            pltpu.VMEM((HIST + tb, 3 * GROUP_W), F32),
            pltpu.VMEM((HIST + tb, P_RWKV_W), F32),
            pltpu.VMEM((HIST + tb, GROUP_W), F32),
            pltpu.VMEM((GROUP_W, GROUP_W), F32),
            pltpu.VMEM((GROUP_W, GROUP_W), F32),
            pltpu.VMEM((GROUP_W, GLA_K), F32),
            pltpu.VMEM((tb, 4 * GROUP_W), F32),
        ],
        compiler_params=pltpu.CompilerParams(
            dimension_semantics=("parallel", "arbitrary"),
            vmem_limit_bytes=56 * 1024 * 1024),
        name="hybrid_layer",
    )(x, prew, win, gconv, vec, mu, w2, scw, gup, wout, postw)


def _pad_rows(a, rows):
    return jnp.pad(a, ((0, rows - a.shape[0]), (0, 0)))


def _prep_layer(w_in, gdn_conv_w, gdn_a_log, gdn_dt_bias, gdn_norm_w, rwkv_mu, rwkv_w0, rwkv_w_up,
                rwkv_a0, rwkv_a_up, rwkv_k_k, rwkv_k_a, rwkv_r_k, rwkv_ln_w, rwkv_ln_b, sc_conv_w,
                gla_a_up, gla_a_bias, gla_norm_w):
    g0 = _GDN0
    gdn = [w_in[:, g0:g0 + 4 * GROUP_W],
           jnp.repeat(w_in[:, g0 + 4 * GROUP_W:g0 + 4 * GROUP_W + N_HEADS], HEAD_V, axis=1),
           jnp.repeat(w_in[:, g0 + 4 * GROUP_W + N_HEADS:g0 + _GDN_COLS], HEAD_V, axis=1)]
    rw = [w_in[:, _RWKV0:_RWKV0 + _RWKV_COLS]]
    sc = [w_in[:, _SC0:_SC0 + _SC_COLS]]
    gla_w = 2 * GLA_K + 2 * GROUP_W
    gla = [w_in[:, _GLA0:_GLA0 + gla_w],
           jnp.pad(w_in[:, _GLA0 + gla_w:_GLA0 + gla_w + GLA_RANK], ((0, 0), (0, LANE - GLA_RANK)))]
    win = jnp.concatenate(gdn + rw + sc + gla, axis=1).astype(BF16)

    def per_head(a):
        return jnp.repeat(a, HEAD_V)

    def per_dim(a):
        return jnp.tile(a, N_HEADS)

    rows = [None] * 12
    rows[V_GDN_ALOG] = per_head(gdn_a_log)
    rows[V_GDN_DT] = per_head(gdn_dt_bias)
    rows[V_GDN_NW] = per_dim(gdn_norm_w)
    rows[V_RW_W0] = rwkv_w0
    rows[V_RW_A0] = rwkv_a0
    rows[V_RW_KK] = rwkv_k_k
    rows[V_RW_KA] = rwkv_k_a
    rows[V_RW_RK] = rwkv_r_k
    rows[V_RW_LNW] = rwkv_ln_w
    rows[V_RW_LNB] = rwkv_ln_b
    rows[V_GLA_NW] = per_dim(gla_norm_w)
    rows[V_GLA_BIAS] = jnp.pad(gla_a_bias, (0, GROUP_W - GLA_K))
    vec = _pad_rows(jnp.stack(rows), N_VEC_ROWS)

    w2 = jnp.zeros((2 * RWKV_RANK, 2 * GROUP_W), F32)
    w2 = w2.at[:RWKV_RANK, :GROUP_W].set(rwkv_w_up).at[RWKV_RANK:, GROUP_W:].set(rwkv_a_up)
    gup = _pad_rows(gla_a_up, LANE)
    return (win, _pad_rows(gdn_conv_w, HIST), vec, rwkv_mu[None, :], w2.astype(BF16),
            _pad_rows(sc_conv_w, HIST), gup.astype(BF16))


def _time_block(seq):
    for tb in (256, 128, 64):
        if seq % tb == 0:
            return tb
    raise ValueError("sequence length must be a multiple of the chunk length")


def kernel(x, pre_norm_w, w_in, gdn_conv_w, gdn_a_log, gdn_dt_bias, gdn_norm_w, rwkv_mu, rwkv_w0,
           rwkv_w_up, rwkv_a0, rwkv_a_up, rwkv_k_k, rwkv_k_a, rwkv_r_k, rwkv_ln_w, rwkv_ln_b,
           sc_conv_w, gla_a_up, gla_a_bias, gla_norm_w, w_out, post_norm_w):
    tb = _time_block(x.shape[1])
    for l in range(w_in.shape[0]):
        win, gconv, vec, mu, w2, scw, gup = _prep_layer(
            w_in[l], gdn_conv_w[l], gdn_a_log[l], gdn_dt_bias[l], gdn_norm_w[l], rwkv_mu[l],
            rwkv_w0[l], rwkv_w_up[l], rwkv_a0[l], rwkv_a_up[l], rwkv_k_k[l], rwkv_k_a[l],
            rwkv_r_k[l], rwkv_ln_w[l], rwkv_ln_b[l], sc_conv_w[l], gla_a_up[l], gla_a_bias[l],
            gla_norm_w[l])
        x = _layer(x, pre_norm_w[l][None, :], win, gconv, vec, mu, w2, scw, gup,
                   w_out[l].astype(BF16), post_norm_w[l][None, :], tb=tb)
    return x
```

```python
import functools
import math

import jax
import jax.numpy as jnp
from jax import lax
from jax.experimental import pallas as pl
from jax.experimental.pallas import tpu as pltpu

F32 = jnp.float32
BF16 = jnp.bfloat16

D_MODEL = 1024
GROUP_W = 256
HEAD_V = 64
N_HEADS = 4
GDN_CONV = 4
RWKV_RANK = 64
RWKV_GN_EPS = 64e-5
SC_CONV = 3
GLA_HEAD_K = 32
GLA_K = 128
GLA_RANK = 16
GLA_TAU = 16.0
CHUNK = 64
EPS = 1e-6
LANE = 128
HIST = 8
GROUP_CHUNKS = 4

_GDN0 = 0
_GDN_COLS = 4 * GROUP_W + 2 * N_HEADS
_RWKV0 = _GDN0 + _GDN_COLS
_RWKV_COLS = 4 * GROUP_W + 2 * RWKV_RANK
_SC0 = _RWKV0 + _RWKV_COLS
_SC_COLS = 4 * GROUP_W
_GLA0 = _SC0 + _SC_COLS

P_GDN = 0
P_GDN_W = 6 * GROUP_W
P_RWKV = P_GDN + P_GDN_W
P_RWKV_W = 4 * GROUP_W + 2 * RWKV_RANK
P_SC = P_RWKV + P_RWKV_W
P_SC_W = 4 * GROUP_W
P_GLA = P_SC + P_SC_W
P_GLA_W = 2 * GLA_K + 2 * GROUP_W + LANE
P_TOTAL = P_GLA + P_GLA_W

(V_GDN_ALOG, V_GDN_DT, V_GDN_NW, V_RW_W0, V_RW_A0, V_RW_KK, V_RW_KA, V_RW_RK, V_RW_LNW, V_RW_LNB,
 V_GLA_NW, V_GLA_BIAS) = range(12)
N_VEC_ROWS = 16


def _mm(a, b):
    return jnp.dot(a.astype(BF16), b.astype(BF16), preferred_element_type=F32)


def _mm_nt(a, b):
    return lax.dot_general(a.astype(BF16), b.astype(BF16), (((1,), (1,)), ((), ())),
                           preferred_element_type=F32)


def _mm_tn(a, b):
    return lax.dot_general(a.astype(BF16), b.astype(BF16), (((0,), (0,)), ((), ())),
                           preferred_element_type=F32)


def _split(a):
    hi = a.astype(BF16)
    lo = (a - hi.astype(F32)).astype(BF16)
    return hi, lo


def _mm_x(a, b_exact):
    hi, lo = _split(a)
    return (jnp.dot(hi, b_exact, preferred_element_type=F32)
            + jnp.dot(lo, b_exact, preferred_element_type=F32))


def _xmm(a_exact, b):
    hi, lo = _split(b)
    return (jnp.dot(a_exact, hi, preferred_element_type=F32)
            + jnp.dot(a_exact, lo, preferred_element_type=F32))


def _silu(x):
    return x * jax.nn.sigmoid(x)


def _softplus(x):
    return jnp.maximum(x, 0.0) + jnp.log1p(jnp.exp(-jnp.abs(x)))


def _iota(shape, dim):
    return lax.broadcasted_iota(jnp.int32, shape, dim)


class _Masks:
    def __init__(self):
        row = _iota((CHUNK, GROUP_W), 0)
        col = _iota((CHUNK, GROUP_W), 1) & (CHUNK - 1)
        self.incl = row >= col
        self.strict = row > col
        self.eye = row == col
        r2 = _iota((GROUP_W, GROUP_W), 0)
        c2 = _iota((GROUP_W, GROUP_W), 1)
        self.bd = (r2 // HEAD_V) == (c2 // HEAD_V)
        self.ones_bd = jnp.where(self.bd, 1.0, 0.0).astype(BF16)
        r3 = _iota((GROUP_W, GLA_K), 0)
        c3 = _iota((GROUP_W, GLA_K), 1)
        self.bd_gla = (r3 // HEAD_V) == (c3 // GLA_HEAD_K)
        r4 = _iota((CHUNK, CHUNK), 0)
        c4 = _iota((CHUNK, CHUNK), 1)
        self.tril = jnp.where(r4 >= c4, 1.0, 0.0).astype(BF16)


def _bd(x, mask):
    x = x.astype(BF16)
    return jnp.where(mask, jnp.concatenate([x] * N_HEADS, axis=0), jnp.zeros((), BF16))


def _head_sum(x, m):
    return _mm_x(x, m.ones_bd)


def _cumsum_chunk(x, m):
    return _xmm(m.tril, x)


def _mm_heads(a, b, m):
    return jnp.dot(a.astype(BF16), _bd(b, m.bd), preferred_element_type=F32)


def _round_robin(*gens):
    live = list(gens)
    while live:
        for g in tuple(live):
            try:
                next(g)
            except StopIteration:
                live.remove(g)


def _inverse_unit_lower(ns, m):
    eye = jnp.where(m.eye, 1.0, 0.0)
    ps = [eye + n for n in ns]
    pws = [_mm_heads(n, n, m) for n in ns]
    yield
    for _ in range(4):
        both = [_mm_heads(jnp.concatenate([pw, p], axis=0), pw, m)
                for pw, p in zip(pws, ps)]
        ps = [p + b[CHUNK:] for p, b in zip(ps, both)]
        pws = [b[:CHUNK] for b in both]
        yield
    return [p + _mm_heads(p, pw, m) for p, pw in zip(ps, pws)]


def _gdn_prep(q, k, v, beta, gc, m):
    g_row = jnp.sum(jnp.where(m.eye, gc, 0.0), axis=0, keepdims=True)
    g_last = gc[CHUNK - 1:CHUNK, :]
    decay = jnp.where(m.incl, jnp.exp(jnp.where(m.incl, gc - g_row, 0.0)), 0.0)
    kb = k * beta
    aq = _mm_nt(jnp.concatenate([kb, q], axis=0), _bd(k, m.bd))
    e_gc = jnp.exp(gc)
    return dict(n=-jnp.where(m.strict, aq[:CHUNK] * decay, 0.0), attn=aq[CHUNK:] * decay,
                vb=v * beta, kbg=kb * e_gc, qg=q * e_gc, k_end=k * jnp.exp(g_last - gc),
                e_last=jnp.exp(g_last))


def _rwkv_prep(r, k, v, kk, a, wlog, cw, m):
    cw_last = cw[CHUNK - 1:CHUNK, :]
    e_ncw = jnp.exp(-cw)
    e_end = jnp.exp(cw_last - cw)
    b = kk * a
    lhs = jnp.concatenate([-kk * jnp.exp(cw - wlog), r * jnp.exp(cw)], axis=0)
    m_b = _mm_nt(lhs, _bd(b * e_ncw, m.bd))
    m_k = _mm_nt(lhs, _bd(k * e_ncw, m.bd))
    a_kv = jnp.concatenate([jnp.where(m.strict, m_k[:CHUNK], 0.0),
                            jnp.where(m.incl, m_k[CHUNK:], 0.0)], axis=0)
    return dict(n=jnp.where(m.strict, m_b[:CHUNK], 0.0), a_rb=jnp.where(m.incl, m_b[CHUNK:], 0.0),
                lhs=lhs, a_kv=a_kv, v=v, ends=jnp.concatenate([b * e_end, k * e_end], axis=0),
                e_last=jnp.exp(cw_last))


def _gla_prep(q, k, v, bc, m):
    b_last = bc[CHUNK - 1:CHUNK, :]
    qe = q * (GLA_HEAD_K ** -0.5) * jnp.exp(bc)
    ke = k * jnp.exp(-bc)
    attn = jnp.where(m.incl, _mm_nt(qe, _bd(ke, m.bd_gla)), 0.0)
    return dict(qe=qe, attn=attn, v=v, k_end=k * jnp.exp(b_last - bc), e_last=jnp.exp(b_last))


def _prep_group(blk, group, m, pre):
    sl = {c: slice(c * CHUNK, (c + 1) * CHUNK) for c in group}
    gc = {c: _cumsum_chunk(blk["gg"][sl[c]], m) for c in group}
    cw = {c: _cumsum_chunk(blk["wlog"][sl[c]], m) for c in group}
    bc = {c: _cumsum_chunk(blk["loga"][sl[c]], m) for c in group}
    yield
    gdn = {c: _gdn_prep(blk["gq"][sl[c]], blk["gk"][sl[c]], blk["gv"][sl[c]], blk["gbeta"][sl[c]],
                        gc[c], m) for c in group}
    rw = {c: _rwkv_prep(blk["rr"][sl[c]], blk["rk"][sl[c]], blk["rv"][sl[c]], blk["rkk"][sl[c]],
                        blk["ra"][sl[c]], blk["wlog"][sl[c]], cw[c], m) for c in group}
    gla = {c: _gla_prep(blk["aq"][sl[c]], blk["ak"][sl[c]], blk["av"][sl[c]], bc[c], m)
           for c in group}
    yield
    for c in group:
        rw[c]["kv"] = _mm(rw[c]["a_kv"], _bd(rw[c]["v"], m.bd))
        gla[c]["intra"] = _mm(gla[c]["attn"], _bd(gla[c]["v"], m.bd))
    t_inv = yield from _inverse_unit_lower([gdn[c]["n"] for c in group]
                                           + [rw[c]["n"] for c in group], m)
    yield
    for i, c in enumerate(group):
        gdn[c]["u"] = _mm_heads(t_inv[i], gdn[c]["vb"], m)
        gdn[c]["w"] = _mm_heads(t_inv[i], gdn[c]["kbg"], m)
        rw[c]["t_inv"] = t_inv[len(group) + i]
        pre[c] = dict(gdn=gdn[c], rw=rw[c], gla=gla[c])
    yield


def _step_group(group, pre, st, ybuf, m):
    for c in group:
        sl = slice(c * CHUNK, (c + 1) * CHUNK)
        g, r, a = pre[c]["gdn"], pre[c]["rw"], pre[c]["gla"]
        wq_s = _mm(jnp.concatenate([g["w"], g["qg"]], axis=0), st["gdn"])
        hs = _mm_nt(r["lhs"], st["rw"])
        ybuf[sl, 3 * GROUP_W:] = _mm_nt(a["qe"], st["gla"]) + a["intra"]
        st["gla"] = (st["gla"] * a["e_last"]
                     + jnp.where(m.bd_gla, _mm_tn(a["v"], a["k_end"]), 0.0))
        yield
        v_new = g["u"] - wq_s[:CHUNK]
        ybuf[sl, 0:GROUP_W] = wq_s[CHUNK:] + _mm(g["attn"], _bd(v_new, m.bd))
        st["gdn"] = st["gdn"] * g["e_last"] + jnp.where(m.bd, _mm_tn(g["k_end"], v_new), 0.0)
        u = _mm_heads(r["t_inv"], hs[:CHUNK] + r["kv"][:CHUNK], m)
        yield
        ybuf[sl, GROUP_W:2 * GROUP_W] = (hs[CHUNK:] + r["kv"][CHUNK:]
                                         + _mm(r["a_rb"], _bd(u, m.bd)))
        upd = _mm_tn(jnp.concatenate([u, r["v"]], axis=0), r["ends"])
        st["rw"] = st["rw"] * r["e_last"] + jnp.where(m.bd, upd, 0.0)
        yield


def _layer_kernel(x_ref, prew_ref, win_ref, gconv_ref, vec_ref, mu_ref, w2_ref, scw_ref, gup_ref,
                  wout_ref, postw_ref, o_ref,
                  ghist, rhist, shist, s_gdn, s_rw, s_gla, ybuf, *, tb):
    t_idx = pl.program_id(1)

    @pl.when(t_idx == 0)
    def _():
        ghist[0:HIST, :] = jnp.zeros((HIST, ghist.shape[1]), F32)
        rhist[0:HIST, :] = jnp.zeros((HIST, rhist.shape[1]), F32)
        shist[0:HIST, :] = jnp.zeros((HIST, shist.shape[1]), F32)
        s_gdn[...] = jnp.zeros_like(s_gdn)
        s_rw[...] = jnp.zeros_like(s_rw)
        s_gla[...] = jnp.zeros_like(s_gla)

    m = _Masks()

    def vec(row):
        return vec_ref[row:row + 1, :]

    x = x_ref[0]
    h = x * lax.rsqrt(jnp.mean(x * x, axis=-1, keepdims=True) + EPS) * prew_ref[...]
    h = h.astype(BF16)

    def proj(start, width):
        return jnp.dot(h, win_ref[:, start:start + width], preferred_element_type=F32)

    pg = proj(P_GDN, P_GDN_W)
    ghist[HIST:HIST + tb, :] = pg[:, :3 * GROUP_W]
    qkv = ghist[pl.ds(HIST - GDN_CONV + 1, tb), :] * gconv_ref[0:1, :]
    for i in range(1, GDN_CONV):
        qkv = qkv + ghist[pl.ds(HIST - GDN_CONV + 1 + i, tb), :] * gconv_ref[i:i + 1, :]
    ghist[0:HIST, :] = ghist[tb:tb + HIST, :]
    qkv = _silu(qkv)
    gq = qkv[:, :GROUP_W]
    gk = qkv[:, GROUP_W:2 * GROUP_W]
    gv = qkv[:, 2 * GROUP_W:]
    gq = gq * lax.rsqrt(_head_sum(gq * gq, m) + EPS) * (HEAD_V ** -0.5)
    gk = gk * lax.rsqrt(_head_sum(gk * gk, m) + EPS)
    gz = pg[:, 3 * GROUP_W:4 * GROUP_W]
    gg = -jnp.exp(vec(V_GDN_ALOG)) * _softplus(pg[:, 4 * GROUP_W:5 * GROUP_W] + vec(V_GDN_DT))
    gbeta = jax.nn.sigmoid(pg[:, 5 * GROUP_W:])

    pr = proj(P_RWKV, P_RWKV_W)
    rhist[HIST:HIST + tb, :] = pr
    prev = rhist[pl.ds(HIST - 1, tb), :]
    rhist[0:HIST, :] = rhist[tb:tb + HIST, :]
    pr = pr + mu_ref[0:1, :] * (prev - pr)
    rr = pr[:, :GROUP_W]
    rk = pr[:, GROUP_W:2 * GROUP_W]
    rv = pr[:, 2 * GROUP_W:3 * GROUP_W]
    rz = pr[:, 3 * GROUP_W:4 * GROUP_W]
    low = pr[:, 4 * GROUP_W:]
    low = jnp.where(_iota(low.shape, 1) < RWKV_RANK, jnp.tanh(low), low)
    up = _mm(low, w2_ref[...])
    wlog = -math.exp(-0.5) * jax.nn.sigmoid(vec(V_RW_W0) + up[:, :GROUP_W])
    ra = jax.nn.sigmoid(vec(V_RW_A0) + up[:, GROUP_W:])
    rkk = rk * vec(V_RW_KK)
    rkk = rkk * lax.rsqrt(_head_sum(rkk * rkk, m) + EPS)
    rk = rk * (1.0 + (ra - 1.0) * vec(V_RW_KA))
    bonus = _head_sum(rr * rk * vec(V_RW_RK), m) * rv

    ps = proj(P_SC, P_SC_W)
    shist[HIST:HIST + tb, :] = ps[:, GROUP_W:2 * GROUP_W] * ps[:, 2 * GROUP_W:3 * GROUP_W]
    cv = shist[pl.ds(HIST - SC_CONV + 1, tb), :] * scw_ref[0:1, :]
    for i in range(1, SC_CONV):
        cv = cv + shist[pl.ds(HIST - SC_CONV + 1 + i, tb), :] * scw_ref[i:i + 1, :]
    shist[0:HIST, :] = shist[tb:tb + HIST, :]
    ybuf[:, 2 * GROUP_W:3 * GROUP_W] = ps[:, :GROUP_W] * cv * _silu(ps[:, 3 * GROUP_W:])

    pa = proj(P_GLA, P_GLA_W)
    aq = pa[:, :GLA_K]
    ak = pa[:, GLA_K:2 * GLA_K]
    av = pa[:, 2 * GLA_K:2 * GLA_K + GROUP_W]
    az = pa[:, 2 * GLA_K + GROUP_W:2 * GLA_K + 2 * GROUP_W]
    a_dn = pa[:, 2 * GLA_K + 2 * GROUP_W:]
    pre = _mm(a_dn, gup_ref[...]) + vec_ref[V_GLA_BIAS:V_GLA_BIAS + 1, :GLA_K]
    loga = -_softplus(-pre) * (1.0 / GLA_TAU)

    blk = dict(gq=gq, gk=gk, gv=gv, gg=gg, gbeta=gbeta, rr=rr, rk=rk, rv=rv, rkk=rkk, ra=ra,
               wlog=wlog, aq=aq, ak=ak, av=av, loga=loga)
    n_chunks = tb // CHUNK
    groups = [list(range(i, min(i + GROUP_CHUNKS, n_chunks)))
              for i in range(0, n_chunks, GROUP_CHUNKS)]
    pre = {}
    st = dict(gdn=s_gdn[...], rw=s_rw[...], gla=s_gla[...])
    _round_robin(_prep_group(blk, groups[0], m, pre))
    for gi, group in enumerate(groups):
        gens = [_step_group(group, pre, st, ybuf, m)]
        if gi + 1 < len(groups):
            gens.append(_prep_group(blk, groups[gi + 1], m, pre))
        _round_robin(*gens)
    s_gdn[...] = st["gdn"]
    s_rw[...] = st["rw"]
    s_gla[...] = st["gla"]

    o = ybuf[:, 0:GROUP_W]
    o = o * lax.rsqrt(_head_sum(o * o, m) * (1.0 / HEAD_V) + EPS) * vec(V_GDN_NW)
    ybuf[:, 0:GROUP_W] = o * _silu(gz)
    y = ybuf[:, GROUP_W:2 * GROUP_W]
    yc = y - _head_sum(y, m) * (1.0 / HEAD_V)
    var = _head_sum(yc * yc, m) * (1.0 / HEAD_V)
    yn = yc * lax.rsqrt(var + RWKV_GN_EPS) * vec(V_RW_LNW) + vec(V_RW_LNB)
    ybuf[:, GROUP_W:2 * GROUP_W] = (yn + bonus) * _silu(rz)
    o = ybuf[:, 3 * GROUP_W:]
    o = o * lax.rsqrt(_head_sum(o * o, m) * (1.0 / HEAD_V) + EPS) * vec(V_GLA_NW)
    ybuf[:, 3 * GROUP_W:] = o * _silu(az)

    out = jnp.dot(ybuf[...].astype(BF16), wout_ref[...], preferred_element_type=F32)
    out = out * lax.rsqrt(jnp.mean(out * out, axis=-1, keepdims=True) + EPS) * postw_ref[...]
    o_ref[0] = x + out


def _layer(x, prew, win, gconv, vec, mu, w2, scw, gup, wout, postw, *, tb):
    bsz, seq, _ = x.shape
    const = lambda b, t: (0, 0)
    full = lambda a: pl.BlockSpec(a.shape, const)
    return pl.pallas_call(
        functools.partial(_layer_kernel, tb=tb),
        grid=(bsz, seq // tb),
        in_specs=[pl.BlockSpec((1, tb, D_MODEL), lambda b, t: (b, t, 0))]
        + [full(a) for a in (prew, win, gconv, vec, mu, w2, scw, gup, wout, postw)],
        out_specs=pl.BlockSpec((1, tb, D_MODEL), lambda b, t: (b, t, 0)),
        out_shape=jax.ShapeDtypeStruct(x.shape, F32),
        scratch_shapes=[
            pltpu.VMEM((HIST + tb, 3 * GROUP_W), F32),
            pltpu.VMEM((HIST + tb, P_RWKV_W), F32),
            pltpu.VMEM((HIST + tb, GROUP_W), F32),
            pltpu.VMEM((GROUP_W, GROUP_W), F32),
            pltpu.VMEM((GROUP_W, GROUP_W), F32),
            pltpu.VMEM((GROUP_W, GLA_K), F32),
            pltpu.VMEM((tb, 4 * GROUP_W), F32),
        ],
        compiler_params=pltpu.CompilerParams(
            dimension_semantics=("parallel", "arbitrary"),
            vmem_limit_bytes=56 * 1024 * 1024),
        name="hybrid_layer",
    )(x, prew, win, gconv, vec, mu, w2, scw, gup, wout, postw)


def _pad_rows(a, rows):
    return jnp.pad(a, ((0, rows - a.shape[0]), (0, 0)))


def _prep_layer(w_in, gdn_conv_w, gdn_a_log, gdn_dt_bias, gdn_norm_w, rwkv_mu, rwkv_w0, rwkv_w_up,
                rwkv_a0, rwkv_a_up, rwkv_k_k, rwkv_k_a, rwkv_r_k, rwkv_ln_w, rwkv_ln_b, sc_conv_w,
                gla_a_up, gla_a_bias, gla_norm_w):
    g0 = _GDN0
    gdn = [w_in[:, g0:g0 + 4 * GROUP_W],
           jnp.repeat(w_in[:, g0 + 4 * GROUP_W:g0 + 4 * GROUP_W + N_HEADS], HEAD_V, axis=1),
           jnp.repeat(w_in[:, g0 + 4 * GROUP_W + N_HEADS:g0 + _GDN_COLS], HEAD_V, axis=1)]
    rw = [w_in[:, _RWKV0:_RWKV0 + _RWKV_COLS]]
    sc = [w_in[:, _SC0:_SC0 + _SC_COLS]]
    gla_w = 2 * GLA_K + 2 * GROUP_W
    gla = [w_in[:, _GLA0:_GLA0 + gla_w],
           jnp.pad(w_in[:, _GLA0 + gla_w:_GLA0 + gla_w + GLA_RANK], ((0, 0), (0, LANE - GLA_RANK)))]
    win = jnp.concatenate(gdn + rw + sc + gla, axis=1).astype(BF16)

    def per_head(a):
        return jnp.repeat(a, HEAD_V)

    def per_dim(a):
        return jnp.tile(a, N_HEADS)

    rows = [None] * 12
    rows[V_GDN_ALOG] = per_head(gdn_a_log)
    rows[V_GDN_DT] = per_head(gdn_dt_bias)
    rows[V_GDN_NW] = per_dim(gdn_norm_w)
    rows[V_RW_W0] = rwkv_w0
    rows[V_RW_A0] = rwkv_a0
    rows[V_RW_KK] = rwkv_k_k
    rows[V_RW_KA] = rwkv_k_a
    rows[V_RW_RK] = rwkv_r_k
    rows[V_RW_LNW] = rwkv_ln_w
    rows[V_RW_LNB] = rwkv_ln_b
    rows[V_GLA_NW] = per_dim(gla_norm_w)
    rows[V_GLA_BIAS] = jnp.pad(gla_a_bias, (0, GROUP_W - GLA_K))
    vec = _pad_rows(jnp.stack(rows), N_VEC_ROWS)

    w2 = jnp.zeros((2 * RWKV_RANK, 2 * GROUP_W), F32)
    w2 = w2.at[:RWKV_RANK, :GROUP_W].set(rwkv_w_up).at[RWKV_RANK:, GROUP_W:].set(rwkv_a_up)
    gup = _pad_rows(gla_a_up, LANE)
    return (win, _pad_rows(gdn_conv_w, HIST), vec, rwkv_mu[None, :], w2.astype(BF16),
            _pad_rows(sc_conv_w, HIST), gup.astype(BF16))


def _time_block(seq):
    for tb in (512, 256, 128, 64):
        if seq % tb == 0:
            return tb
    raise ValueError("sequence length must be a multiple of the chunk length")


def kernel(x, pre_norm_w, w_in, gdn_conv_w, gdn_a_log, gdn_dt_bias, gdn_norm_w, rwkv_mu, rwkv_w0,
           rwkv_w_up, rwkv_a0, rwkv_a_up, rwkv_k_k, rwkv_k_a, rwkv_r_k, rwkv_ln_w, rwkv_ln_b,
           sc_conv_w, gla_a_up, gla_a_bias, gla_norm_w, w_out, post_norm_w):
    tb = _time_block(x.shape[1])
    for l in range(w_in.shape[0]):
        win, gconv, vec, mu, w2, scw, gup = _prep_layer(
            w_in[l], gdn_conv_w[l], gdn_a_log[l], gdn_dt_bias[l], gdn_norm_w[l], rwkv_mu[l],
            rwkv_w0[l], rwkv_w_up[l], rwkv_a0[l], rwkv_a_up[l], rwkv_k_k[l], rwkv_k_a[l],
            rwkv_r_k[l], rwkv_ln_w[l], rwkv_ln_b[l], sc_conv_w[l], gla_a_up[l], gla_a_bias[l],
            gla_norm_w[l])
        x = _layer(x, pre_norm_w[l][None, :], win, gconv, vec, mu, w2, scw, gup,
                   w_out[l].astype(BF16), post_norm_w[l][None, :], tb=tb)
    return x
```

```python
import functools
import math

import jax
import jax.numpy as jnp
from jax import lax
from jax.experimental import pallas as pl
from jax.experimental.pallas import tpu as pltpu

F32 = jnp.float32
BF16 = jnp.bfloat16

D_MODEL = 1024
GROUP_W = 256
HEAD_V = 64
N_HEADS = 4
GDN_CONV = 4
RWKV_RANK = 64
RWKV_GN_EPS = 64e-5
SC_CONV = 3
GLA_HEAD_K = 32
GLA_K = 128
GLA_RANK = 16
GLA_TAU = 16.0
CHUNK = 64
EPS = 1e-6
LANE = 128
HIST = 8
GROUP_CHUNKS = 4

_GDN0 = 0
_GDN_COLS = 4 * GROUP_W + 2 * N_HEADS
_RWKV0 = _GDN0 + _GDN_COLS
_RWKV_COLS = 4 * GROUP_W + 2 * RWKV_RANK
_SC0 = _RWKV0 + _RWKV_COLS
_SC_COLS = 4 * GROUP_W
_GLA0 = _SC0 + _SC_COLS

P_GDN = 0
P_GDN_W = 6 * GROUP_W
P_RWKV = P_GDN + P_GDN_W
P_RWKV_W = 4 * GROUP_W + 2 * RWKV_RANK
P_SC = P_RWKV + P_RWKV_W
P_SC_W = 4 * GROUP_W
P_GLA = P_SC + P_SC_W
P_GLA_W = 2 * GLA_K + 2 * GROUP_W + LANE
P_TOTAL = P_GLA + P_GLA_W

(V_GDN_ALOG, V_GDN_DT, V_GDN_NW, V_RW_W0, V_RW_A0, V_RW_KK, V_RW_KA, V_RW_RK, V_RW_LNW, V_RW_LNB,
 V_GLA_NW, V_GLA_BIAS) = range(12)
N_VEC_ROWS = 16


def _mm(a, b):
    return jnp.dot(a.astype(BF16), b.astype(BF16), preferred_element_type=F32)


def _mm_nt(a, b):
    return lax.dot_general(a.astype(BF16), b.astype(BF16), (((1,), (1,)), ((), ())),
                           preferred_element_type=F32)


def _mm_tn(a, b):
    return lax.dot_general(a.astype(BF16), b.astype(BF16), (((0,), (0,)), ((), ())),
                           preferred_element_type=F32)


def _split(a):
    hi = a.astype(BF16)
    lo = (a - hi.astype(F32)).astype(BF16)
    return hi, lo


def _xmm(a_exact, b):
    hi, lo = _split(b)
    return (jnp.dot(a_exact, hi, preferred_element_type=F32)
            + jnp.dot(a_exact, lo, preferred_element_type=F32))


def _silu(x):
    return x * jax.nn.sigmoid(x)


def _softplus(x):
    return jnp.maximum(x, 0.0) + jnp.log1p(jnp.exp(-jnp.abs(x)))


def _iota(shape, dim):
    return lax.broadcasted_iota(jnp.int32, shape, dim)


class _Masks:
    def __init__(self):
        row = _iota((CHUNK, GROUP_W), 0)
        col = _iota((CHUNK, GROUP_W), 1) & (CHUNK - 1)
        self.incl = row >= col
        self.strict = row > col
        self.eye = row == col
        r2 = _iota((GROUP_W, GROUP_W), 0)
        c2 = _iota((GROUP_W, GROUP_W), 1)
        self.bd = (r2 // HEAD_V) == (c2 // HEAD_V)
        self.ones_bd = jnp.where(self.bd, 1.0, 0.0).astype(BF16)
        r3 = _iota((GROUP_W, GLA_K), 0)
        c3 = _iota((GROUP_W, GLA_K), 1)
        self.bd_gla = (r3 // HEAD_V) == (c3 // GLA_HEAD_K)
        r4 = _iota((CHUNK, CHUNK), 0)
        c4 = _iota((CHUNK, CHUNK), 1)
        self.tril = jnp.where(r4 >= c4, 1.0, 0.0).astype(BF16)


def _bd(x, mask):
    x = x.astype(BF16)
    return jnp.where(mask, jnp.concatenate([x] * N_HEADS, axis=0), jnp.zeros((), BF16))


def _head_sum(x, m):
    return jnp.dot(x.astype(BF16), m.ones_bd, preferred_element_type=F32)


def _cumsum_chunk(x, m):
    return _xmm(m.tril, x)


def _mm_heads(a, b, m):
    return jnp.dot(a.astype(BF16), _bd(b, m.bd), preferred_element_type=F32)


def _round_robin(*gens):
    live = list(gens)
    while live:
        for g in tuple(live):
            try:
                next(g)
            except StopIteration:
                live.remove(g)


def _inverse_unit_lower(ns, m):
    eye = jnp.where(m.eye, 1.0, 0.0)
    ps = [eye + n for n in ns]
    pws = [_mm_heads(n, n, m) for n in ns]
    yield
    for _ in range(4):
        both = [_mm_heads(jnp.concatenate([pw, p], axis=0), pw, m)
                for pw, p in zip(pws, ps)]
        ps = [p + b[CHUNK:] for p, b in zip(ps, both)]
        pws = [b[:CHUNK] for b in both]
        yield
    return [p + _mm_heads(p, pw, m) for p, pw in zip(ps, pws)]


def _gdn_prep(q, k, v, beta, gc, m):
    g_row = jnp.sum(jnp.where(m.eye, gc, 0.0), axis=0, keepdims=True)
    g_last = gc[CHUNK - 1:CHUNK, :]
    decay = jnp.where(m.incl, jnp.exp(jnp.where(m.incl, gc - g_row, 0.0)), 0.0)
    kb = k * beta
    aq = _mm_nt(jnp.concatenate([kb, q], axis=0), _bd(k, m.bd))
    e_gc = jnp.exp(gc)
    return dict(n=-jnp.where(m.strict, aq[:CHUNK] * decay, 0.0), attn=aq[CHUNK:] * decay,
                vb=v * beta, kbg=kb * e_gc, qg=q * e_gc, k_end=k * jnp.exp(g_last - gc),
                e_last=jnp.exp(g_last))


def _rwkv_prep(r, k, v, kk, a, wlog, cw, m):
    cw_last = cw[CHUNK - 1:CHUNK, :]
    e_ncw = jnp.exp(-cw)
    e_end = jnp.exp(cw_last - cw)
    b = kk * a
    lhs = jnp.concatenate([-kk * jnp.exp(cw - wlog), r * jnp.exp(cw)], axis=0)
    m_b = _mm_nt(lhs, _bd(b * e_ncw, m.bd))
    m_k = _mm_nt(lhs, _bd(k * e_ncw, m.bd))
    a_kv = jnp.concatenate([jnp.where(m.strict, m_k[:CHUNK], 0.0),
                            jnp.where(m.incl, m_k[CHUNK:], 0.0)], axis=0)
    return dict(n=jnp.where(m.strict, m_b[:CHUNK], 0.0), a_rb=jnp.where(m.incl, m_b[CHUNK:], 0.0),
                lhs=lhs, a_kv=a_kv, v=v, ends=jnp.concatenate([b * e_end, k * e_end], axis=0),
                e_last=jnp.exp(cw_last))


def _gla_prep(q, k, v, bc, m):
    b_last = bc[CHUNK - 1:CHUNK, :]
    qe = q * (GLA_HEAD_K ** -0.5) * jnp.exp(bc)
    ke = k * jnp.exp(-bc)
    attn = jnp.where(m.incl, _mm_nt(qe, _bd(ke, m.bd_gla)), 0.0)
    return dict(qe=qe, attn=attn, v=v, k_end=k * jnp.exp(b_last - bc), e_last=jnp.exp(b_last))


def _prep_group(blk, group, m, pre):
    sl = {c: slice(c * CHUNK, (c + 1) * CHUNK) for c in group}
    gc = {c: _cumsum_chunk(blk["gg"][sl[c]], m) for c in group}
    cw = {c: _cumsum_chunk(blk["wlog"][sl[c]], m) for c in group}
    bc = {c: _cumsum_chunk(blk["loga"][sl[c]], m) for c in group}
    yield
    gdn = {c: _gdn_prep(blk["gq"][sl[c]], blk["gk"][sl[c]], blk["gv"][sl[c]], blk["gbeta"][sl[c]],
                        gc[c], m) for c in group}
    rw = {c: _rwkv_prep(blk["rr"][sl[c]], blk["rk"][sl[c]], blk["rv"][sl[c]], blk["rkk"][sl[c]],
                        blk["ra"][sl[c]], blk["wlog"][sl[c]], cw[c], m) for c in group}
    gla = {c: _gla_prep(blk["aq"][sl[c]], blk["ak"][sl[c]], blk["av"][sl[c]], bc[c], m)
           for c in group}
    yield
    for c in group:
        rw[c]["kv"] = _mm(rw[c]["a_kv"], _bd(rw[c]["v"], m.bd))
        gla[c]["intra"] = _mm(gla[c]["attn"], _bd(gla[c]["v"], m.bd))
    t_inv = yield from _inverse_unit_lower([gdn[c]["n"] for c in group]
                                           + [rw[c]["n"] for c in group], m)
    yield
    for i, c in enumerate(group):
        gdn[c]["u"] = _mm_heads(t_inv[i], gdn[c]["vb"], m)
        gdn[c]["w"] = _mm_heads(t_inv[i], gdn[c]["kbg"], m)
        rw[c]["t_inv"] = t_inv[len(group) + i]
        pre[c] = dict(gdn=gdn[c], rw=rw[c], gla=gla[c])
    yield


def _step_group(group, pre, st, ybuf, m):
    for c in group:
        sl = slice(c * CHUNK, (c + 1) * CHUNK)
        g, r, a = pre[c]["gdn"], pre[c]["rw"], pre[c]["gla"]
        wq_s = _mm(jnp.concatenate([g["w"], g["qg"]], axis=0), st["gdn"])
        hs = _mm_nt(r["lhs"], st["rw"])
        ybuf[sl, 3 * GROUP_W:] = _mm_nt(a["qe"], st["gla"]) + a["intra"]
        st["gla"] = (st["gla"] * a["e_last"]
                     + jnp.where(m.bd_gla, _mm_tn(a["v"], a["k_end"]), 0.0))
        yield
        v_new = g["u"] - wq_s[:CHUNK]
        ybuf[sl, 0:GROUP_W] = wq_s[CHUNK:] + _mm(g["attn"], _bd(v_new, m.bd))
        st["gdn"] = st["gdn"] * g["e_last"] + jnp.where(m.bd, _mm_tn(g["k_end"], v_new), 0.0)
        u = _mm_heads(r["t_inv"], hs[:CHUNK] + r["kv"][:CHUNK], m)
        yield
        ybuf[sl, GROUP_W:2 * GROUP_W] = (hs[CHUNK:] + r["kv"][CHUNK:]
                                         + _mm(r["a_rb"], _bd(u, m.bd)))
        upd = _mm_tn(jnp.concatenate([u, r["v"]], axis=0), r["ends"])
        st["rw"] = st["rw"] * r["e_last"] + jnp.where(m.bd, upd, 0.0)
        yield


def _layer_kernel(x_ref, prew_ref, win_ref, gconv_ref, vec_ref, mu_ref, w2_ref, scw_ref, gup_ref,
                  wout_ref, postw_ref, o_ref,
                  ghist, rhist, shist, s_gdn, s_rw, s_gla, ybuf, *, tb):
    t_idx = pl.program_id(1)

    @pl.when(t_idx == 0)
    def _():
        ghist[0:HIST, :] = jnp.zeros((HIST, ghist.shape[1]), F32)
        rhist[0:HIST, :] = jnp.zeros((HIST, rhist.shape[1]), F32)
        shist[0:HIST, :] = jnp.zeros((HIST, shist.shape[1]), F32)
        s_gdn[...] = jnp.zeros_like(s_gdn)
        s_rw[...] = jnp.zeros_like(s_rw)
        s_gla[...] = jnp.zeros_like(s_gla)

    m = _Masks()

    def vec(row):
        return vec_ref[row:row + 1, :]

    x = x_ref[0]
    h = x * lax.rsqrt(jnp.mean(x * x, axis=-1, keepdims=True) + EPS) * prew_ref[...]
    h = h.astype(BF16)
    blk = {}

    def proj(start, width):
        return jnp.dot(h, win_ref[:, start:start + width], preferred_element_type=F32)

    def gdn_front():
        pg = proj(P_GDN, P_GDN_W)
        yield
        ghist[HIST:HIST + tb, :] = pg[:, :3 * GROUP_W]
        qkv = ghist[pl.ds(HIST - GDN_CONV + 1, tb), :] * gconv_ref[0:1, :]
        for i in range(1, GDN_CONV):
            qkv = qkv + ghist[pl.ds(HIST - GDN_CONV + 1 + i, tb), :] * gconv_ref[i:i + 1, :]
        ghist[0:HIST, :] = ghist[tb:tb + HIST, :]
        qkv = _silu(qkv)
        gq = qkv[:, :GROUP_W]
        gk = qkv[:, GROUP_W:2 * GROUP_W]
        q_ss = _head_sum(gq * gq, m)
        k_ss = _head_sum(gk * gk, m)
        yield
        blk["gq"] = gq * lax.rsqrt(q_ss + EPS) * (HEAD_V ** -0.5)
        blk["gk"] = gk * lax.rsqrt(k_ss + EPS)
        blk["gv"] = qkv[:, 2 * GROUP_W:]
        blk["gz"] = pg[:, 3 * GROUP_W:4 * GROUP_W]
        blk["gg"] = -jnp.exp(vec(V_GDN_ALOG)) * _softplus(
            pg[:, 4 * GROUP_W:5 * GROUP_W] + vec(V_GDN_DT))
        blk["gbeta"] = jax.nn.sigmoid(pg[:, 5 * GROUP_W:])
        yield

    def rwkv_front():
        pr = proj(P_RWKV, P_RWKV_W)
        yield
        rhist[HIST:HIST + tb, :] = pr
        prev = rhist[pl.ds(HIST - 1, tb), :]
        rhist[0:HIST, :] = rhist[tb:tb + HIST, :]
        pr = pr + mu_ref[0:1, :] * (prev - pr)
        rk = pr[:, GROUP_W:2 * GROUP_W]
        low = pr[:, 4 * GROUP_W:]
        low = jnp.where(_iota(low.shape, 1) < RWKV_RANK, jnp.tanh(low), low)
        up = _mm(low, w2_ref[...])
        rkk = rk * vec(V_RW_KK)
        kk_ss = _head_sum(rkk * rkk, m)
        yield
        ra = jax.nn.sigmoid(vec(V_RW_A0) + up[:, GROUP_W:])
        rr = pr[:, :GROUP_W]
        rk = rk * (1.0 + (ra - 1.0) * vec(V_RW_KA))
        rk_sum = _head_sum(rr * rk * vec(V_RW_RK), m)
        blk["wlog"] = -math.exp(-0.5) * jax.nn.sigmoid(vec(V_RW_W0) + up[:, :GROUP_W])
        blk["rkk"] = rkk * lax.rsqrt(kk_ss + EPS)
        blk["ra"], blk["rr"], blk["rk"] = ra, rr, rk
        blk["rv"] = pr[:, 2 * GROUP_W:3 * GROUP_W]
        blk["rz"] = pr[:, 3 * GROUP_W:4 * GROUP_W]
        yield
        blk["bonus"] = rk_sum * blk["rv"]
        yield

    def sc_gla_front():
        ps = proj(P_SC, P_SC_W)
        pa = proj(P_GLA, P_GLA_W)
        yield
        shist[HIST:HIST + tb, :] = ps[:, GROUP_W:2 * GROUP_W] * ps[:, 2 * GROUP_W:3 * GROUP_W]
        cv = shist[pl.ds(HIST - SC_CONV + 1, tb), :] * scw_ref[0:1, :]
        for i in range(1, SC_CONV):
            cv = cv + shist[pl.ds(HIST - SC_CONV + 1 + i, tb), :] * scw_ref[i:i + 1, :]
        shist[0:HIST, :] = shist[tb:tb + HIST, :]
        ybuf[:, 2 * GROUP_W:3 * GROUP_W] = ps[:, :GROUP_W] * cv * _silu(ps[:, 3 * GROUP_W:])
        a_dn = pa[:, 2 * GLA_K + 2 * GROUP_W:]
        pre_gate = _mm(a_dn, gup_ref[...]) + vec_ref[V_GLA_BIAS:V_GLA_BIAS + 1, :GLA_K]
        yield
        blk["loga"] = -_softplus(-pre_gate) * (1.0 / GLA_TAU)
        blk["aq"] = pa[:, :GLA_K]
        blk["ak"] = pa[:, GLA_K:2 * GLA_K]
        blk["av"] = pa[:, 2 * GLA_K:2 * GLA_K + GROUP_W]
        blk["az"] = pa[:, 2 * GLA_K + GROUP_W:2 * GLA_K + 2 * GROUP_W]
        yield

    def finish(group):
        rows = slice(group[0] * CHUNK, (group[-1] + 1) * CHUNK)
        o_g = ybuf[rows, 0:GROUP_W]
        y_r = ybuf[rows, GROUP_W:2 * GROUP_W]
        o_a = ybuf[rows, 3 * GROUP_W:]
        g_ss = _head_sum(o_g * o_g, m)
        r_mean = _head_sum(y_r, m) * (1.0 / HEAD_V)
        a_ss = _head_sum(o_a * o_a, m)
        yield
        yc = y_r - r_mean
        var = _head_sum(yc * yc, m) * (1.0 / HEAD_V)
        o_g = o_g * lax.rsqrt(g_ss * (1.0 / HEAD_V) + EPS) * vec(V_GDN_NW)
        ybuf[rows, 0:GROUP_W] = o_g * _silu(blk["gz"][rows])
        o_a = o_a * lax.rsqrt(a_ss * (1.0 / HEAD_V) + EPS) * vec(V_GLA_NW)
        ybuf[rows, 3 * GROUP_W:] = o_a * _silu(blk["az"][rows])
        yield
        yn = yc * lax.rsqrt(var + RWKV_GN_EPS) * vec(V_RW_LNW) + vec(V_RW_LNB)
        ybuf[rows, GROUP_W:2 * GROUP_W] = (yn + blk["bonus"][rows]) * _silu(blk["rz"][rows])
        out = jnp.dot(ybuf[rows, :].astype(BF16), wout_ref[...], preferred_element_type=F32)
        yield
        out = out * lax.rsqrt(jnp.mean(out * out, axis=-1, keepdims=True) + EPS) * postw_ref[...]
        o_ref[0, rows, :] = x[rows] + out
        yield

    _round_robin(gdn_front(), rwkv_front(), sc_gla_front())

    n_chunks = tb // CHUNK
    groups = [list(range(i, min(i + GROUP_CHUNKS, n_chunks)))
              for i in range(0, n_chunks, GROUP_CHUNKS)]
    pre = {}
    st = dict(gdn=s_gdn[...], rw=s_rw[...], gla=s_gla[...])
    _round_robin(_prep_group(blk, groups[0], m, pre))
    for gi, group in enumerate(groups):
        gens = [_step_group(group, pre, st, ybuf, m)]
        if gi + 1 < len(groups):
            gens.append(_prep_group(blk, groups[gi + 1], m, pre))
        if gi > 0:
            gens.append(finish(groups[gi - 1]))
        _round_robin(*gens)
    _round_robin(finish(groups[-1]))
    s_gdn[...] = st["gdn"]
    s_rw[...] = st["rw"]
    s_gla[...] = st["gla"]


def _layer(x, prew, win, gconv, vec, mu, w2, scw, gup, wout, postw, *, tb):
    bsz, seq, _ = x.shape
    const = lambda b, t: (0, 0)
    full = lambda a: pl.BlockSpec(a.shape, const)
    return pl.pallas_call(
        functools.partial(_layer_kernel, tb=tb),
        grid=(bsz, seq // tb),
        in_specs=[pl.BlockSpec((1, tb, D_MODEL), lambda b, t: (b, t, 0))]
        + [full(a) for a in (prew, win, gconv, vec, mu, w2, scw, gup, wout, postw)],
        out_specs=pl.BlockSpec((1, tb, D_MODEL), lambda b, t: (b, t, 0)),
        out_shape=jax.ShapeDtypeStruct(x.shape, F32),
        scratch_shapes=[
            pltpu.VMEM((HIST + tb, 3 * GROUP_W), F32),
            pltpu.VMEM((HIST + tb, P_RWKV_W), F32),
            pltpu.VMEM((HIST + tb, GROUP_W), F32),
            pltpu.VMEM((GROUP_W, GROUP_W), F32),
            pltpu.VMEM((GROUP_W, GROUP_W), F32),
            pltpu.VMEM((GROUP_W, GLA_K), F32),
            pltpu.VMEM((tb, 4 * GROUP_W), F32),
        ],
        compiler_params=pltpu.CompilerParams(
            dimension_semantics=("parallel", "arbitrary"),
            vmem_limit_bytes=56 * 1024 * 1024),
        name="hybrid_layer",
    )(x, prew, win, gconv, vec, mu, w2, scw, gup, wout, postw)


def _pad_rows(a, rows):
    return jnp.pad(a, ((0, rows - a.shape[0]), (0, 0)))


def _prep_layer(w_in, gdn_conv_w, gdn_a_log, gdn_dt_bias, gdn_norm_w, rwkv_mu, rwkv_w0, rwkv_w_up,
                rwkv_a0, rwkv_a_up, rwkv_k_k, rwkv_k_a, rwkv_r_k, rwkv_ln_w, rwkv_ln_b, sc_conv_w,
                gla_a_up, gla_a_bias, gla_norm_w):
    g0 = _GDN0
    gdn = [w_in[:, g0:g0 + 4 * GROUP_W],
           jnp.repeat(w_in[:, g0 + 4 * GROUP_W:g0 + 4 * GROUP_W + N_HEADS], HEAD_V, axis=1),
           jnp.repeat(w_in[:, g0 + 4 * GROUP_W + N_HEADS:g0 + _GDN_COLS], HEAD_V, axis=1)]
    rw = [w_in[:, _RWKV0:_RWKV0 + _RWKV_COLS]]
    sc = [w_in[:, _SC0:_SC0 + _SC_COLS]]
    gla_w = 2 * GLA_K + 2 * GROUP_W
    gla = [w_in[:, _GLA0:_GLA0 + gla_w],
           jnp.pad(w_in[:, _GLA0 + gla_w:_GLA0 + gla_w + GLA_RANK], ((0, 0), (0, LANE - GLA_RANK)))]
    win = jnp.concatenate(gdn + rw + sc + gla, axis=1).astype(BF16)

    def per_head(a):
        return jnp.repeat(a, HEAD_V)

    def per_dim(a):
        return jnp.tile(a, N_HEADS)

    rows = [None] * 12
    rows[V_GDN_ALOG] = per_head(gdn_a_log)
    rows[V_GDN_DT] = per_head(gdn_dt_bias)
    rows[V_GDN_NW] = per_dim(gdn_norm_w)
    rows[V_RW_W0] = rwkv_w0
    rows[V_RW_A0] = rwkv_a0
    rows[V_RW_KK] = rwkv_k_k
    rows[V_RW_KA] = rwkv_k_a
    rows[V_RW_RK] = rwkv_r_k
    rows[V_RW_LNW] = rwkv_ln_w
    rows[V_RW_LNB] = rwkv_ln_b
    rows[V_GLA_NW] = per_dim(gla_norm_w)
    rows[V_GLA_BIAS] = jnp.pad(gla_a_bias, (0, GROUP_W - GLA_K))
    vec = _pad_rows(jnp.stack(rows), N_VEC_ROWS)

    w2 = jnp.zeros((2 * RWKV_RANK, 2 * GROUP_W), F32)
    w2 = w2.at[:RWKV_RANK, :GROUP_W].set(rwkv_w_up).at[RWKV_RANK:, GROUP_W:].set(rwkv_a_up)
    gup = _pad_rows(gla_a_up, LANE)
    return (win, _pad_rows(gdn_conv_w, HIST), vec, rwkv_mu[None, :], w2.astype(BF16),
            _pad_rows(sc_conv_w, HIST), gup.astype(BF16))


def _time_block(seq):
    for tb in (512, 256, 128, 64):
        if seq % tb == 0:
            return tb
    raise ValueError("sequence length must be a multiple of the chunk length")


def kernel(x, pre_norm_w, w_in, gdn_conv_w, gdn_a_log, gdn_dt_bias, gdn_norm_w, rwkv_mu, rwkv_w0,
           rwkv_w_up, rwkv_a0, rwkv_a_up, rwkv_k_k, rwkv_k_a, rwkv_r_k, rwkv_ln_w, rwkv_ln_b,
           sc_conv_w, gla_a_up, gla_a_bias, gla_norm_w, w_out, post_norm_w):
    tb = _time_block(x.shape[1])
    for l in range(w_in.shape[0]):
        win, gconv, vec, mu, w2, scw, gup = _prep_layer(
            w_in[l], gdn_conv_w[l], gdn_a_log[l], gdn_dt_bias[l], gdn_norm_w[l], rwkv_mu[l],
            rwkv_w0[l], rwkv_w_up[l], rwkv_a0[l], rwkv_a_up[l], rwkv_k_k[l], rwkv_k_a[l],
            rwkv_r_k[l], rwkv_ln_w[l], rwkv_ln_b[l], sc_conv_w[l], gla_a_up[l], gla_a_bias[l],
            gla_norm_w[l])
        x = _layer(x, pre_norm_w[l][None, :], win, gconv, vec, mu, w2, scw, gup,
                   w_out[l].astype(BF16), post_norm_w[l][None, :], tb=tb)
    return x
```

```python
import functools
import math

import jax
import jax.numpy as jnp
from jax import lax
from jax.experimental import pallas as pl
from jax.experimental.pallas import tpu as pltpu

F32 = jnp.float32
BF16 = jnp.bfloat16

D_MODEL = 1024
GROUP_W = 256
HEAD_V = 64
N_HEADS = 4
GDN_CONV = 4
RWKV_RANK = 64
RWKV_GN_EPS = 64e-5
SC_CONV = 3
GLA_HEAD_K = 32
GLA_K = 128
GLA_RANK = 16
GLA_TAU = 16.0
CHUNK = 64
EPS = 1e-6
LANE = 128
HIST = 8
GROUP_CHUNKS = 3

_GDN0 = 0
_GDN_COLS = 4 * GROUP_W + 2 * N_HEADS
_RWKV0 = _GDN0 + _GDN_COLS
_RWKV_COLS = 4 * GROUP_W + 2 * RWKV_RANK
_SC0 = _RWKV0 + _RWKV_COLS
_SC_COLS = 4 * GROUP_W
_GLA0 = _SC0 + _SC_COLS

P_GDN = 0
P_GDN_W = 4 * GROUP_W
P_RWKV = P_GDN + P_GDN_W
P_RWKV_W = 4 * GROUP_W
P_SC = P_RWKV + P_RWKV_W
P_SC_W = 4 * GROUP_W
P_GLA = P_SC + P_SC_W
P_GLA_W = 2 * GLA_K + 2 * GROUP_W
P_LOW = P_GLA + P_GLA_W
P_LOW_W = 2 * LANE
P_TOTAL = P_LOW + P_LOW_W
L_RW_W = 0
L_RW_A = L_RW_W + RWKV_RANK
L_GLA_A = L_RW_A + RWKV_RANK
L_GDN_ALPHA = L_GLA_A + GLA_RANK
L_GDN_BETA = L_GDN_ALPHA + N_HEADS
S_RW_UP = 0
S_GLA = S_RW_UP + 2 * GROUP_W
S_GDN = S_GLA + GLA_K
S_TOTAL = S_GDN + 2 * GROUP_W

(V_GDN_ALOG, V_GDN_DT, V_GDN_NW, V_RW_W0, V_RW_A0, V_RW_KK, V_RW_KA, V_RW_RK, V_RW_LNW, V_RW_LNB,
 V_GLA_NW, V_GLA_BIAS) = range(12)
N_VEC_ROWS = 16


def _mm(a, b):
    return jnp.dot(a.astype(BF16), b.astype(BF16), preferred_element_type=F32)


def _mm_nt(a, b):
    return lax.dot_general(a.astype(BF16), b.astype(BF16), (((1,), (1,)), ((), ())),
                           preferred_element_type=F32)


def _mm_tn(a, b):
    return lax.dot_general(a.astype(BF16), b.astype(BF16), (((0,), (0,)), ((), ())),
                           preferred_element_type=F32)


def _split(a):
    hi = a.astype(BF16)
    lo = (a - hi.astype(F32)).astype(BF16)
    return hi, lo


def _xmm(a_exact, b):
    hi, lo = _split(b)
    return (jnp.dot(a_exact, hi, preferred_element_type=F32)
            + jnp.dot(a_exact, lo, preferred_element_type=F32))


def _silu(x):
    return x * jax.nn.sigmoid(x)


def _softplus(x):
    return jnp.maximum(x, 0.0) + jnp.log1p(jnp.exp(-jnp.abs(x)))


def _iota(shape, dim):
    return lax.broadcasted_iota(jnp.int32, shape, dim)


class _Masks:
    def __init__(self):
        row = _iota((CHUNK, GROUP_W), 0)
        col = _iota((CHUNK, GROUP_W), 1) & (CHUNK - 1)
        self.incl = row >= col
        self.strict = row > col
        self.eye = row == col
        r2 = _iota((GROUP_W, GROUP_W), 0)
        c2 = _iota((GROUP_W, GROUP_W), 1)
        self.bd = (r2 // HEAD_V) == (c2 // HEAD_V)
        self.ones_bd = jnp.where(self.bd, 1.0, 0.0).astype(BF16)
        r3 = _iota((GROUP_W, GLA_K), 0)
        c3 = _iota((GROUP_W, GLA_K), 1)
        self.bd_gla = (r3 // HEAD_V) == (c3 // GLA_HEAD_K)
        r4 = _iota((CHUNK, CHUNK), 0)
        c4 = _iota((CHUNK, CHUNK), 1)
        self.tril = jnp.where(r4 >= c4, 1.0, 0.0).astype(BF16)


def _bd(x, mask):
    x = x.astype(BF16)
    return jnp.where(mask, jnp.concatenate([x] * N_HEADS, axis=0), jnp.zeros((), BF16))


def _head_sum(x, m):
    return jnp.dot(x.astype(BF16), m.ones_bd, preferred_element_type=F32)


def _cumsum_chunk(x, m):
    return _xmm(m.tril, x)


def _mm_heads(a, b, m):
    return jnp.dot(a.astype(BF16), _bd(b, m.bd), preferred_element_type=F32)


def _round_robin(*gens):
    live = list(gens)
    while live:
        for g in tuple(live):
            try:
                next(g)
            except StopIteration:
                live.remove(g)


def _inverse_unit_lower(ns, m):
    eye = jnp.where(m.eye, 1.0, 0.0)
    ps = [eye + n for n in ns]
    pws = [_mm_heads(n, n, m) for n in ns]
    yield
    for _ in range(4):
        both = [_mm_heads(jnp.concatenate([pw, p], axis=0), pw, m)
                for pw, p in zip(pws, ps)]
        ps = [p + b[CHUNK:] for p, b in zip(ps, both)]
        pws = [b[:CHUNK] for b in both]
        yield
    return [p + _mm_heads(p, pw, m) for p, pw in zip(ps, pws)]


def _gdn_prep(q, k, v, beta, gc, m):
    g_row = jnp.sum(jnp.where(m.eye, gc, 0.0), axis=0, keepdims=True)
    g_last = gc[CHUNK - 1:CHUNK, :]
    decay = jnp.where(m.incl, jnp.exp(jnp.where(m.incl, gc - g_row, 0.0)), 0.0)
    kb = k * beta
    aq = _mm_nt(jnp.concatenate([kb, q], axis=0), _bd(k, m.bd))
    e_gc = jnp.exp(gc)
    return dict(n=-jnp.where(m.strict, aq[:CHUNK] * decay, 0.0), attn=aq[CHUNK:] * decay,
                vb=v * beta, kbg=kb * e_gc, qg=q * e_gc, k_end=k * jnp.exp(g_last - gc),
                e_last=jnp.exp(g_last))


def _rwkv_prep(r, k, v, kk, a, wlog, cw, m):
    cw_last = cw[CHUNK - 1:CHUNK, :]
    e_ncw = jnp.exp(-cw)
    e_end = jnp.exp(cw_last - cw)
    b = kk * a
    lhs = jnp.concatenate([-kk * jnp.exp(cw - wlog), r * jnp.exp(cw)], axis=0)
    m_b = _mm_nt(lhs, _bd(b * e_ncw, m.bd))
    m_k = _mm_nt(lhs, _bd(k * e_ncw, m.bd))
    a_kv = jnp.concatenate([jnp.where(m.strict, m_k[:CHUNK], 0.0),
                            jnp.where(m.incl, m_k[CHUNK:], 0.0)], axis=0)
    return dict(n=jnp.where(m.strict, m_b[:CHUNK], 0.0), a_rb=jnp.where(m.incl, m_b[CHUNK:], 0.0),
                lhs=lhs, a_kv=a_kv, v=v, ends=jnp.concatenate([b * e_end, k * e_end], axis=0),
                e_last=jnp.exp(cw_last))


def _gla_prep(q, k, v, bc, m):
    b_last = bc[CHUNK - 1:CHUNK, :]
    qe = q * (GLA_HEAD_K ** -0.5) * jnp.exp(bc)
    ke = k * jnp.exp(-bc)
    attn = jnp.where(m.incl, _mm_nt(qe, _bd(ke, m.bd_gla)), 0.0)
    return dict(qe=qe, attn=attn, v=v, k_end=k * jnp.exp(b_last - bc), e_last=jnp.exp(b_last))


def _prep_group(blk, group, m, pre):
    sl = {c: slice(c * CHUNK, (c + 1) * CHUNK) for c in group}
    gc = {c: _cumsum_chunk(blk["gg"][sl[c]], m) for c in group}
    cw = {c: _cumsum_chunk(blk["wlog"][sl[c]], m) for c in group}
    bc = {c: _cumsum_chunk(blk["loga"][sl[c]], m) for c in group}
    yield
    gdn = {c: _gdn_prep(blk["gq"][sl[c]], blk["gk"][sl[c]], blk["gv"][sl[c]], blk["gbeta"][sl[c]],
                        gc[c], m) for c in group}
    rw = {c: _rwkv_prep(blk["rr"][sl[c]], blk["rk"][sl[c]], blk["rv"][sl[c]], blk["rkk"][sl[c]],
                        blk["ra"][sl[c]], blk["wlog"][sl[c]], cw[c], m) for c in group}
    gla = {c: _gla_prep(blk["aq"][sl[c]], blk["ak"][sl[c]], blk["av"][sl[c]], bc[c], m)
           for c in group}
    yield
    for c in group:
        rw[c]["kv"] = _mm(rw[c]["a_kv"], _bd(rw[c]["v"], m.bd))
        gla[c]["intra"] = _mm(gla[c]["attn"], _bd(gla[c]["v"], m.bd))
    t_inv = yield from _inverse_unit_lower([gdn[c]["n"] for c in group]
                                           + [rw[c]["n"] for c in group], m)
    yield
    for i, c in enumerate(group):
        gdn[c]["u"] = _mm_heads(t_inv[i], gdn[c]["vb"], m)
        gdn[c]["w"] = _mm_heads(t_inv[i], gdn[c]["kbg"], m)
        rw[c]["t_inv"] = t_inv[len(group) + i]
        pre[c] = dict(gdn=gdn[c], rw=rw[c], gla=gla[c])
    yield


def _step_group(group, pre, st, ybuf, m):
    for c in group:
        sl = slice(c * CHUNK, (c + 1) * CHUNK)
        g, r, a = pre[c]["gdn"], pre[c]["rw"], pre[c]["gla"]
        wq_s = _mm(jnp.concatenate([g["w"], g["qg"]], axis=0), st["gdn"])
        hs = _mm_nt(r["lhs"], st["rw"])
        ybuf[sl, 3 * GROUP_W:] = _mm_nt(a["qe"], st["gla"]) + a["intra"]
        st["gla"] = (st["gla"] * a["e_last"]
                     + jnp.where(m.bd_gla, _mm_tn(a["v"], a["k_end"]), 0.0))
        yield
        v_new = g["u"] - wq_s[:CHUNK]
        ybuf[sl, 0:GROUP_W] = wq_s[CHUNK:] + _mm(g["attn"], _bd(v_new, m.bd))
        st["gdn"] = st["gdn"] * g["e_last"] + jnp.where(m.bd, _mm_tn(g["k_end"], v_new), 0.0)
        u = _mm_heads(r["t_inv"], hs[:CHUNK] + r["kv"][:CHUNK], m)
        yield
        ybuf[sl, GROUP_W:2 * GROUP_W] = (hs[CHUNK:] + r["kv"][CHUNK:]
                                         + _mm(r["a_rb"], _bd(u, m.bd)))
        upd = _mm_tn(jnp.concatenate([u, r["v"]], axis=0), r["ends"])
        st["rw"] = st["rw"] * r["e_last"] + jnp.where(m.bd, upd, 0.0)
        yield


def _layer_kernel(x_ref, prew_ref, win_ref, gconv_ref, vec_ref, mu_ref, mulow_ref, wlow_ref, scw_ref,
                  wout_ref, postw_ref, o_ref,
                  ghist, rhist, lhist, shist, s_gdn, s_rw, s_gla, ybuf, *, tb):
    t_idx = pl.program_id(1)

    @pl.when(t_idx == 0)
    def _():
        ghist[0:HIST, :] = jnp.zeros((HIST, ghist.shape[1]), F32)
        rhist[0:HIST, :] = jnp.zeros((HIST, rhist.shape[1]), F32)
        lhist[0:HIST, :] = jnp.zeros((HIST, lhist.shape[1]), F32)
        shist[0:HIST, :] = jnp.zeros((HIST, shist.shape[1]), F32)
        s_gdn[...] = jnp.zeros_like(s_gdn)
        s_rw[...] = jnp.zeros_like(s_rw)
        s_gla[...] = jnp.zeros_like(s_gla)

    m = _Masks()

    def vec(row):
        return vec_ref[row:row + 1, :]

    x = x_ref[0]
    h = x * lax.rsqrt(jnp.mean(x * x, axis=-1, keepdims=True) + EPS) * prew_ref[...]
    h = h.astype(BF16)
    blk = {}

    def proj(start, width):
        return jnp.dot(h, win_ref[:, start:start + width], preferred_element_type=F32)

    def low_front():
        low = proj(P_LOW, P_LOW_W)
        yield
        lhist[HIST:HIST + tb, :] = low
        prev = lhist[pl.ds(HIST - 1, tb), :]
        lhist[0:HIST, :] = lhist[tb:tb + HIST, :]
        low = low + mulow_ref[0:1, :] * (prev - low)
        lane = _iota(low.shape, 1)
        low = jnp.where((lane >= L_RW_W) & (lane < L_RW_A), jnp.tanh(low), low)
        blk["small"] = _mm(low, wlow_ref[...])
        yield

    def gdn_front():
        pg = proj(P_GDN, P_GDN_W)
        yield
        ghist[HIST:HIST + tb, :] = pg[:, :3 * GROUP_W]
        qkv = ghist[pl.ds(HIST - GDN_CONV + 1, tb), :] * gconv_ref[0:1, :]
        for i in range(1, GDN_CONV):
            qkv = qkv + ghist[pl.ds(HIST - GDN_CONV + 1 + i, tb), :] * gconv_ref[i:i + 1, :]
        ghist[0:HIST, :] = ghist[tb:tb + HIST, :]
        qkv = _silu(qkv)
        gq = qkv[:, :GROUP_W]
        gk = qkv[:, GROUP_W:2 * GROUP_W]
        q_ss = _head_sum(gq * gq, m)
        k_ss = _head_sum(gk * gk, m)
        yield
        blk["gq"] = gq * lax.rsqrt(q_ss + EPS) * (HEAD_V ** -0.5)
        blk["gk"] = gk * lax.rsqrt(k_ss + EPS)
        blk["gv"] = qkv[:, 2 * GROUP_W:]
        blk["gz"] = pg[:, 3 * GROUP_W:4 * GROUP_W]
        small = blk["small"]
        blk["gg"] = -jnp.exp(vec(V_GDN_ALOG)) * _softplus(
            small[:, S_GDN:S_GDN + GROUP_W] + vec(V_GDN_DT))
        blk["gbeta"] = jax.nn.sigmoid(small[:, S_GDN + GROUP_W:S_GDN + 2 * GROUP_W])
        yield

    def rwkv_front():
        pr = proj(P_RWKV, P_RWKV_W)
        yield
        rhist[HIST:HIST + tb, :] = pr
        prev = rhist[pl.ds(HIST - 1, tb), :]
        rhist[0:HIST, :] = rhist[tb:tb + HIST, :]
        pr = pr + mu_ref[0:1, :] * (prev - pr)
        rk = pr[:, GROUP_W:2 * GROUP_W]
        rkk = rk * vec(V_RW_KK)
        kk_ss = _head_sum(rkk * rkk, m)
        yield
        up = blk["small"][:, S_RW_UP:S_RW_UP + 2 * GROUP_W]
        ra = jax.nn.sigmoid(vec(V_RW_A0) + up[:, GROUP_W:])
        rr = pr[:, :GROUP_W]
        rk = rk * (1.0 + (ra - 1.0) * vec(V_RW_KA))
        rk_sum = _head_sum(rr * rk * vec(V_RW_RK), m)
        blk["wlog"] = -math.exp(-0.5) * jax.nn.sigmoid(vec(V_RW_W0) + up[:, :GROUP_W])
        blk["rkk"] = rkk * lax.rsqrt(kk_ss + EPS)
        blk["ra"], blk["rr"], blk["rk"] = ra, rr, rk
        blk["rv"] = pr[:, 2 * GROUP_W:3 * GROUP_W]
        blk["rz"] = pr[:, 3 * GROUP_W:4 * GROUP_W]
        yield
        blk["bonus"] = rk_sum * blk["rv"]
        yield

    def sc_gla_front():
        ps = proj(P_SC, P_SC_W)
        pa = proj(P_GLA, P_GLA_W)
        yield
        shist[HIST:HIST + tb, :] = ps[:, GROUP_W:2 * GROUP_W] * ps[:, 2 * GROUP_W:3 * GROUP_W]
        cv = shist[pl.ds(HIST - SC_CONV + 1, tb), :] * scw_ref[0:1, :]
        for i in range(1, SC_CONV):
            cv = cv + shist[pl.ds(HIST - SC_CONV + 1 + i, tb), :] * scw_ref[i:i + 1, :]
        shist[0:HIST, :] = shist[tb:tb + HIST, :]
        ybuf[:, 2 * GROUP_W:3 * GROUP_W] = ps[:, :GROUP_W] * cv * _silu(ps[:, 3 * GROUP_W:])
        yield
        pre_gate = (blk["small"][:, S_GLA:S_GLA + GLA_K]
                    + vec_ref[V_GLA_BIAS:V_GLA_BIAS + 1, :GLA_K])
        blk["loga"] = -_softplus(-pre_gate) * (1.0 / GLA_TAU)
        blk["aq"] = pa[:, :GLA_K]
        blk["ak"] = pa[:, GLA_K:2 * GLA_K]
        blk["av"] = pa[:, 2 * GLA_K:2 * GLA_K + GROUP_W]
        blk["az"] = pa[:, 2 * GLA_K + GROUP_W:2 * GLA_K + 2 * GROUP_W]
        yield

    def finish(group):
        rows = slice(group[0] * CHUNK, (group[-1] + 1) * CHUNK)
        o_g = ybuf[rows, 0:GROUP_W]
        y_r = ybuf[rows, GROUP_W:2 * GROUP_W]
        o_a = ybuf[rows, 3 * GROUP_W:]
        g_ss = _head_sum(o_g * o_g, m)
        r_mean = _head_sum(y_r, m) * (1.0 / HEAD_V)
        a_ss = _head_sum(o_a * o_a, m)
        yield
        yc = y_r - r_mean
        var = _head_sum(yc * yc, m) * (1.0 / HEAD_V)
        o_g = o_g * lax.rsqrt(g_ss * (1.0 / HEAD_V) + EPS) * vec(V_GDN_NW)
        ybuf[rows, 0:GROUP_W] = o_g * _silu(blk["gz"][rows])
        o_a = o_a * lax.rsqrt(a_ss * (1.0 / HEAD_V) + EPS) * vec(V_GLA_NW)
        ybuf[rows, 3 * GROUP_W:] = o_a * _silu(blk["az"][rows])
        yield
        yn = yc * lax.rsqrt(var + RWKV_GN_EPS) * vec(V_RW_LNW) + vec(V_RW_LNB)
        ybuf[rows, GROUP_W:2 * GROUP_W] = (yn + blk["bonus"][rows]) * _silu(blk["rz"][rows])
        out = jnp.dot(ybuf[rows, :].astype(BF16), wout_ref[...], preferred_element_type=F32)
        yield
        out = out * lax.rsqrt(jnp.mean(out * out, axis=-1, keepdims=True) + EPS) * postw_ref[...]
        o_ref[0, rows, :] = x[rows] + out
        yield

    _round_robin(low_front(), gdn_front(), rwkv_front(), sc_gla_front())

    n_chunks = tb // CHUNK
    groups = [list(range(i, min(i + GROUP_CHUNKS, n_chunks)))
              for i in range(0, n_chunks, GROUP_CHUNKS)]
    pre = {}
    st = dict(gdn=s_gdn[...], rw=s_rw[...], gla=s_gla[...])
    _round_robin(_prep_group(blk, groups[0], m, pre))
    for gi, group in enumerate(groups):
        gens = [_step_group(group, pre, st, ybuf, m)]
        if gi + 1 < len(groups):
            gens.append(_prep_group(blk, groups[gi + 1], m, pre))
        if gi > 0:
            gens.append(finish(groups[gi - 1]))
        _round_robin(*gens)
    _round_robin(finish(groups[-1]))
    s_gdn[...] = st["gdn"]
    s_rw[...] = st["rw"]
    s_gla[...] = st["gla"]


def _layer(x, prew, win, gconv, vec, mu, mulow, wlow, scw, wout, postw, *, tb):
    bsz, seq, _ = x.shape
    const = lambda b, t: (0, 0)
    full = lambda a: pl.BlockSpec(a.shape, const)
    return pl.pallas_call(
        functools.partial(_layer_kernel, tb=tb),
        grid=(bsz, seq // tb),
        in_specs=[pl.BlockSpec((1, tb, D_MODEL), lambda b, t: (b, t, 0))]
        + [full(a) for a in (prew, win, gconv, vec, mu, mulow, wlow, scw, wout, postw)],
        out_specs=pl.BlockSpec((1, tb, D_MODEL), lambda b, t: (b, t, 0)),
        out_shape=jax.ShapeDtypeStruct(x.shape, F32),
        scratch_shapes=[
            pltpu.VMEM((HIST + tb, 3 * GROUP_W), F32),
            pltpu.VMEM((HIST + tb, P_RWKV_W), F32),
            pltpu.VMEM((HIST + tb, P_LOW_W), F32),
            pltpu.VMEM((HIST + tb, GROUP_W), F32),
            pltpu.VMEM((GROUP_W, GROUP_W), F32),
            pltpu.VMEM((GROUP_W, GROUP_W), F32),
            pltpu.VMEM((GROUP_W, GLA_K), F32),
            pltpu.VMEM((tb, 4 * GROUP_W), F32),
        ],
        compiler_params=pltpu.CompilerParams(
            dimension_semantics=("parallel", "arbitrary"),
            vmem_limit_bytes=56 * 1024 * 1024),
        name="hybrid_layer",
    )(x, prew, win, gconv, vec, mu, mulow, wlow, scw, wout, postw)


def _pad_rows(a, rows):
    return jnp.pad(a, ((0, rows - a.shape[0]), (0, 0)))


def _prep_layer(w_in, gdn_conv_w, gdn_a_log, gdn_dt_bias, gdn_norm_w, rwkv_mu, rwkv_w0, rwkv_w_up,
                rwkv_a0, rwkv_a_up, rwkv_k_k, rwkv_k_a, rwkv_r_k, rwkv_ln_w, rwkv_ln_b, sc_conv_w,
                gla_a_up, gla_a_bias, gla_norm_w):
    g0, r0, a0 = _GDN0, _RWKV0, _GLA0
    gla_w = 2 * GLA_K + 2 * GROUP_W
    narrow = [w_in[:, r0 + 4 * GROUP_W:r0 + _RWKV_COLS],
              w_in[:, a0 + gla_w:a0 + gla_w + GLA_RANK],
              w_in[:, g0 + 4 * GROUP_W:g0 + _GDN_COLS]]
    used = L_GDN_BETA + N_HEADS
    win = jnp.concatenate(
        [w_in[:, g0:g0 + 4 * GROUP_W], w_in[:, r0:r0 + 4 * GROUP_W], w_in[:, _SC0:_SC0 + _SC_COLS],
         w_in[:, a0:a0 + gla_w]] + narrow + [jnp.zeros((D_MODEL, P_LOW_W - used), F32)],
        axis=1).astype(BF16)

    def per_head(a):
        return jnp.repeat(a, HEAD_V)

    def per_dim(a):
        return jnp.tile(a, N_HEADS)

    rows = [None] * 12
    rows[V_GDN_ALOG] = per_head(gdn_a_log)
    rows[V_GDN_DT] = per_head(gdn_dt_bias)
    rows[V_GDN_NW] = per_dim(gdn_norm_w)
    rows[V_RW_W0] = rwkv_w0
    rows[V_RW_A0] = rwkv_a0
    rows[V_RW_KK] = rwkv_k_k
    rows[V_RW_KA] = rwkv_k_a
    rows[V_RW_RK] = rwkv_r_k
    rows[V_RW_LNW] = rwkv_ln_w
    rows[V_RW_LNB] = rwkv_ln_b
    rows[V_GLA_NW] = per_dim(gla_norm_w)
    rows[V_GLA_BIAS] = jnp.pad(gla_a_bias, (0, GROUP_W - GLA_K))
    vec = _pad_rows(jnp.stack(rows), N_VEC_ROWS)

    expand = jnp.repeat(jnp.eye(N_HEADS, dtype=F32), HEAD_V, axis=1)
    wlow = jnp.zeros((P_LOW_W, S_TOTAL), F32)
    wlow = wlow.at[L_RW_W:L_RW_W + RWKV_RANK, S_RW_UP:S_RW_UP + GROUP_W].set(rwkv_w_up)
    wlow = wlow.at[L_RW_A:L_RW_A + RWKV_RANK, S_RW_UP + GROUP_W:S_RW_UP + 2 * GROUP_W].set(rwkv_a_up)
    wlow = wlow.at[L_GLA_A:L_GLA_A + GLA_RANK, S_GLA:S_GLA + GLA_K].set(gla_a_up)
    wlow = wlow.at[L_GDN_ALPHA:L_GDN_ALPHA + N_HEADS, S_GDN:S_GDN + GROUP_W].set(expand)
    wlow = wlow.at[L_GDN_BETA:L_GDN_BETA + N_HEADS, S_GDN + GROUP_W:S_GDN + 2 * GROUP_W].set(expand)
    mu_low = jnp.pad(rwkv_mu[4 * GROUP_W:], (0, P_LOW_W - 2 * RWKV_RANK))[None, :]
    return (win, _pad_rows(gdn_conv_w, HIST), vec, rwkv_mu[None, :4 * GROUP_W], mu_low,
            wlow.astype(BF16), _pad_rows(sc_conv_w, HIST))


def _time_block(seq):
    for tb in (512, 256, 128, 64):
        if seq % tb == 0:
            return tb
    raise ValueError("sequence length must be a multiple of the chunk length")


def kernel(x, pre_norm_w, w_in, gdn_conv_w, gdn_a_log, gdn_dt_bias, gdn_norm_w, rwkv_mu, rwkv_w0,
           rwkv_w_up, rwkv_a0, rwkv_a_up, rwkv_k_k, rwkv_k_a, rwkv_r_k, rwkv_ln_w, rwkv_ln_b,
           sc_conv_w, gla_a_up, gla_a_bias, gla_norm_w, w_out, post_norm_w):
    tb = _time_block(x.shape[1])
    for l in range(w_in.shape[0]):
        win, gconv, vec, mu, mulow, wlow, scw = _prep_layer(
            w_in[l], gdn_conv_w[l], gdn_a_log[l], gdn_dt_bias[l], gdn_norm_w[l], rwkv_mu[l],
            rwkv_w0[l], rwkv_w_up[l], rwkv_a0[l], rwkv_a_up[l], rwkv_k_k[l], rwkv_k_a[l],
            rwkv_r_k[l], rwkv_ln_w[l], rwkv_ln_b[l], sc_conv_w[l], gla_a_up[l], gla_a_bias[l],
            gla_norm_w[l])
        x = _layer(x, pre_norm_w[l][None, :], win, gconv, vec, mu, mulow, wlow, scw,
                   w_out[l].astype(BF16), post_norm_w[l][None, :], tb=tb)
    return x
```

```python
import functools
import math

import jax
import jax.numpy as jnp
from jax import lax
from jax.experimental import pallas as pl
from jax.experimental.pallas import tpu as pltpu

F32 = jnp.float32
BF16 = jnp.bfloat16

D_MODEL = 1024
GROUP_W = 256
HEAD_V = 64
N_HEADS = 4
GDN_CONV = 4
RWKV_RANK = 64
RWKV_GN_EPS = 64e-5
SC_CONV = 3
GLA_HEAD_K = 32
GLA_K = 128
GLA_RANK = 16
GLA_TAU = 16.0
CHUNK = 64
EPS = 1e-6
LANE = 128
HIST = 8
GROUP_SIZES = (3, 3, 2)
FRONT_ROWS = 256

_GDN0 = 0
_GDN_COLS = 4 * GROUP_W + 2 * N_HEADS
_RWKV0 = _GDN0 + _GDN_COLS
_RWKV_COLS = 4 * GROUP_W + 2 * RWKV_RANK
_SC0 = _RWKV0 + _RWKV_COLS
_SC_COLS = 4 * GROUP_W
_GLA0 = _SC0 + _SC_COLS

P_GDN = 0
P_GDN_W = 4 * GROUP_W
P_RWKV = P_GDN + P_GDN_W
P_RWKV_W = 4 * GROUP_W
P_SC = P_RWKV + P_RWKV_W
P_SC_W = 4 * GROUP_W
P_GLA = P_SC + P_SC_W
P_GLA_W = 2 * GLA_K + 2 * GROUP_W
P_LOW = P_GLA + P_GLA_W
P_LOW_W = 2 * LANE
P_TOTAL = P_LOW + P_LOW_W
L_RW_W = 0
L_RW_A = L_RW_W + RWKV_RANK
L_GLA_A = L_RW_A + RWKV_RANK
L_GDN_ALPHA = L_GLA_A + GLA_RANK
L_GDN_BETA = L_GDN_ALPHA + N_HEADS
S_RW_UP = 0
S_GLA = S_RW_UP + 2 * GROUP_W
S_GDN = S_GLA + GLA_K
S_TOTAL = S_GDN + 2 * GROUP_W

(V_GDN_ALOG, V_GDN_DT, V_GDN_NW, V_RW_W0, V_RW_A0, V_RW_KK, V_RW_KA, V_RW_RK, V_RW_LNW, V_RW_LNB,
 V_GLA_NW, V_GLA_BIAS) = range(12)
N_VEC_ROWS = 16


def _mm(a, b):
    return jnp.dot(a.astype(BF16), b.astype(BF16), preferred_element_type=F32)


def _mm_nt(a, b):
    return lax.dot_general(a.astype(BF16), b.astype(BF16), (((1,), (1,)), ((), ())),
                           preferred_element_type=F32)


def _mm_tn(a, b):
    return lax.dot_general(a.astype(BF16), b.astype(BF16), (((0,), (0,)), ((), ())),
                           preferred_element_type=F32)


def _split(a):
    hi = a.astype(BF16)
    lo = (a - hi.astype(F32)).astype(BF16)
    return hi, lo


def _xmm(a_exact, b):
    hi, lo = _split(b)
    return (jnp.dot(a_exact, hi, preferred_element_type=F32)
            + jnp.dot(a_exact, lo, preferred_element_type=F32))


def _silu(x):
    return x * jax.nn.sigmoid(x)


def _softplus(x):
    return jnp.maximum(x, 0.0) + jnp.log1p(jnp.exp(-jnp.abs(x)))


def _iota(shape, dim):
    return lax.broadcasted_iota(jnp.int32, shape, dim)


class _Masks:
    def __init__(self):
        row = _iota((CHUNK, GROUP_W), 0)
        col = _iota((CHUNK, GROUP_W), 1) & (CHUNK - 1)
        self.incl = row >= col
        self.strict = row > col
        self.eye = row == col
        r2 = _iota((GROUP_W, GROUP_W), 0)
        c2 = _iota((GROUP_W, GROUP_W), 1)
        self.bd = (r2 // HEAD_V) == (c2 // HEAD_V)
        self.ones_bd = jnp.where(self.bd, 1.0, 0.0).astype(BF16)
        r3 = _iota((GROUP_W, GLA_K), 0)
        c3 = _iota((GROUP_W, GLA_K), 1)
        self.bd_gla = (r3 // HEAD_V) == (c3 // GLA_HEAD_K)
        r4 = _iota((CHUNK, CHUNK), 0)
        c4 = _iota((CHUNK, CHUNK), 1)
        self.tril = jnp.where(r4 >= c4, 1.0, 0.0).astype(BF16)


def _bd(x, mask):
    x = x.astype(BF16)
    return jnp.where(mask, jnp.concatenate([x] * N_HEADS, axis=0), jnp.zeros((), BF16))


def _head_sum(x, m):
    return jnp.dot(x.astype(BF16), m.ones_bd, preferred_element_type=F32)


def _cumsum_chunk(x, m):
    return _xmm(m.tril, x)


def _mm_heads(a, b, m):
    return jnp.dot(a.astype(BF16), _bd(b, m.bd), preferred_element_type=F32)


def _round_robin(*gens):
    live = list(gens)
    while live:
        for g in tuple(live):
            try:
                next(g)
            except StopIteration:
                live.remove(g)


def _inverse_unit_lower(ns, m):
    eye = jnp.where(m.eye, 1.0, 0.0)
    ps = [eye + n for n in ns]
    pws = [_mm_heads(n, n, m) for n in ns]
    yield
    for _ in range(4):
        both = [_mm_heads(jnp.concatenate([pw, p], axis=0), pw, m)
                for pw, p in zip(pws, ps)]
        ps = [p + b[CHUNK:] for p, b in zip(ps, both)]
        pws = [b[:CHUNK] for b in both]
        yield
    return [p + _mm_heads(p, pw, m) for p, pw in zip(ps, pws)]


def _gdn_prep(q, k, v, beta, gc, m):
    g_row = jnp.sum(jnp.where(m.eye, gc, 0.0), axis=0, keepdims=True)
    g_last = gc[CHUNK - 1:CHUNK, :]
    decay = jnp.where(m.incl, jnp.exp(jnp.where(m.incl, gc - g_row, 0.0)), 0.0)
    kb = k * beta
    aq = _mm_nt(jnp.concatenate([kb, q], axis=0), _bd(k, m.bd))
    e_gc = jnp.exp(gc)
    return dict(n=-jnp.where(m.strict, aq[:CHUNK] * decay, 0.0), attn=aq[CHUNK:] * decay,
                vb=v * beta, kbg=kb * e_gc, qg=q * e_gc, k_end=k * jnp.exp(g_last - gc),
                e_last=jnp.exp(g_last))


def _rwkv_prep(r, k, v, kk, a, wlog, cw, m):
    cw_last = cw[CHUNK - 1:CHUNK, :]
    e_ncw = jnp.exp(-cw)
    e_end = jnp.exp(cw_last - cw)
    b = kk * a
    lhs = jnp.concatenate([-kk * jnp.exp(cw - wlog), r * jnp.exp(cw)], axis=0)
    m_b = _mm_nt(lhs, _bd(b * e_ncw, m.bd))
    m_k = _mm_nt(lhs, _bd(k * e_ncw, m.bd))
    a_kv = jnp.concatenate([jnp.where(m.strict, m_k[:CHUNK], 0.0),
                            jnp.where(m.incl, m_k[CHUNK:], 0.0)], axis=0)
    return dict(n=jnp.where(m.strict, m_b[:CHUNK], 0.0), a_rb=jnp.where(m.incl, m_b[CHUNK:], 0.0),
                lhs=lhs, a_kv=a_kv, v=v, ends=jnp.concatenate([b * e_end, k * e_end], axis=0),
                e_last=jnp.exp(cw_last))


def _gla_prep(q, k, v, bc, m):
    b_last = bc[CHUNK - 1:CHUNK, :]
    qe = q * (GLA_HEAD_K ** -0.5) * jnp.exp(bc)
    ke = k * jnp.exp(-bc)
    attn = jnp.where(m.incl, _mm_nt(qe, _bd(ke, m.bd_gla)), 0.0)
    return dict(qe=qe, attn=attn, v=v, k_end=k * jnp.exp(b_last - bc), e_last=jnp.exp(b_last))


class _Rows:
    def __init__(self):
        self.pieces = {}

    def put(self, name, r0, arr):
        self.pieces.setdefault(name, []).append((r0, r0 + arr.shape[0], arr))

    def get(self, name, rows):
        parts = [arr[max(rows.start, r0) - r0:min(rows.stop, r1) - r0]
                 for r0, r1, arr in self.pieces[name] if r0 < rows.stop and rows.start < r1]
        return parts[0] if len(parts) == 1 else jnp.concatenate(parts, axis=0)


def _prep_group(blk, group, m, pre):
    sl = {c: slice(c * CHUNK, (c + 1) * CHUNK) for c in group}
    gc = {c: _cumsum_chunk(blk.get("gg", sl[c]), m) for c in group}
    cw = {c: _cumsum_chunk(blk.get("wlog", sl[c]), m) for c in group}
    bc = {c: _cumsum_chunk(blk.get("loga", sl[c]), m) for c in group}
    yield
    gdn = {c: _gdn_prep(blk.get("gq", sl[c]), blk.get("gk", sl[c]), blk.get("gv", sl[c]), blk.get("gbeta", sl[c]),
                        gc[c], m) for c in group}
    rw = {c: _rwkv_prep(blk.get("rr", sl[c]), blk.get("rk", sl[c]), blk.get("rv", sl[c]), blk.get("rkk", sl[c]),
                        blk.get("ra", sl[c]), blk.get("wlog", sl[c]), cw[c], m) for c in group}
    gla = {c: _gla_prep(blk.get("aq", sl[c]), blk.get("ak", sl[c]), blk.get("av", sl[c]), bc[c], m)
           for c in group}
    yield
    for c in group:
        rw[c]["kv"] = _mm(rw[c]["a_kv"], _bd(rw[c]["v"], m.bd))
        gla[c]["intra"] = _mm(gla[c]["attn"], _bd(gla[c]["v"], m.bd))
    t_inv = yield from _inverse_unit_lower([gdn[c]["n"] for c in group]
                                           + [rw[c]["n"] for c in group], m)
    yield
    for i, c in enumerate(group):
        gdn[c]["u"] = _mm_heads(t_inv[i], gdn[c]["vb"], m)
        gdn[c]["w"] = _mm_heads(t_inv[i], gdn[c]["kbg"], m)
        rw[c]["t_inv"] = t_inv[len(group) + i]
        pre[c] = dict(gdn=gdn[c], rw=rw[c], gla=gla[c])
    yield


def _step_group(group, pre, st, ybuf, m):
    for c in group:
        sl = slice(c * CHUNK, (c + 1) * CHUNK)
        g, r, a = pre[c]["gdn"], pre[c]["rw"], pre[c]["gla"]
        wq_s = _mm(jnp.concatenate([g["w"], g["qg"]], axis=0), st["gdn"])
        hs = _mm_nt(r["lhs"], st["rw"])
        ybuf[sl, 3 * GROUP_W:] = _mm_nt(a["qe"], st["gla"]) + a["intra"]
        st["gla"] = (st["gla"] * a["e_last"]
                     + jnp.where(m.bd_gla, _mm_tn(a["v"], a["k_end"]), 0.0))
        yield
        v_new = g["u"] - wq_s[:CHUNK]
        ybuf[sl, 0:GROUP_W] = wq_s[CHUNK:] + _mm(g["attn"], _bd(v_new, m.bd))
        st["gdn"] = st["gdn"] * g["e_last"] + jnp.where(m.bd, _mm_tn(g["k_end"], v_new), 0.0)
        u = _mm_heads(r["t_inv"], hs[:CHUNK] + r["kv"][:CHUNK], m)
        yield
        ybuf[sl, GROUP_W:2 * GROUP_W] = (hs[CHUNK:] + r["kv"][CHUNK:]
                                         + _mm(r["a_rb"], _bd(u, m.bd)))
        upd = _mm_tn(jnp.concatenate([u, r["v"]], axis=0), r["ends"])
        st["rw"] = st["rw"] * r["e_last"] + jnp.where(m.bd, upd, 0.0)
        yield


def _layer_kernel(x_ref, prew_ref, win_ref, gconv_ref, vec_ref, mu_ref, mulow_ref, wlow_ref, scw_ref,
                  wout_ref, postw_ref, o_ref,
                  ghist, rhist, lhist, shist, s_gdn, s_rw, s_gla, ybuf, *, tb):
    t_idx = pl.program_id(1)

    @pl.when(t_idx == 0)
    def _():
        ghist[0:HIST, :] = jnp.zeros((HIST, ghist.shape[1]), F32)
        rhist[0:HIST, :] = jnp.zeros((HIST, rhist.shape[1]), F32)
        lhist[0:HIST, :] = jnp.zeros((HIST, lhist.shape[1]), F32)
        shist[0:HIST, :] = jnp.zeros((HIST, shist.shape[1]), F32)
        s_gdn[...] = jnp.zeros_like(s_gdn)
        s_rw[...] = jnp.zeros_like(s_rw)
        s_gla[...] = jnp.zeros_like(s_gla)

    m = _Masks()

    def vec(row):
        return vec_ref[row:row + 1, :]

    x = x_ref[0]
    blk = _Rows()

    def fronts(r0, r1, last):
        n = r1 - r0
        rows = slice(r0, r1)
        xr = x[rows]
        h = xr * lax.rsqrt(jnp.mean(xr * xr, axis=-1, keepdims=True) + EPS) * prew_ref[...]
        h = h.astype(BF16)

        def proj(start, width):
            return jnp.dot(h, win_ref[:, start:start + width], preferred_element_type=F32)

        def shifted(hist, back):
            return hist[pl.ds(HIST + r0 - back, n), :]

        def keep_tail(hist):
            if last:
                hist[0:HIST, :] = hist[tb:tb + HIST, :]

        def low_front():
            low = proj(P_LOW, P_LOW_W)
            yield
            lhist[HIST + r0:HIST + r1, :] = low
            prev = shifted(lhist, 1)
            keep_tail(lhist)
            low = low + mulow_ref[0:1, :] * (prev - low)
            lane = _iota(low.shape, 1)
            low = jnp.where((lane >= L_RW_W) & (lane < L_RW_A), jnp.tanh(low), low)
            blk.put("small", r0, _mm(low, wlow_ref[...]))
            yield

        def gdn_front():
            pg = proj(P_GDN, P_GDN_W)
            yield
            ghist[HIST + r0:HIST + r1, :] = pg[:, :3 * GROUP_W]
            qkv = shifted(ghist, GDN_CONV - 1) * gconv_ref[0:1, :]
            for i in range(1, GDN_CONV):
                qkv = qkv + shifted(ghist, GDN_CONV - 1 - i) * gconv_ref[i:i + 1, :]
            keep_tail(ghist)
            qkv = _silu(qkv)
            gq = qkv[:, :GROUP_W]
            gk = qkv[:, GROUP_W:2 * GROUP_W]
            q_ss = _head_sum(gq * gq, m)
            k_ss = _head_sum(gk * gk, m)
            yield
            blk.put("gq", r0, gq * lax.rsqrt(q_ss + EPS) * (HEAD_V ** -0.5))
            blk.put("gk", r0, gk * lax.rsqrt(k_ss + EPS))
            blk.put("gv", r0, qkv[:, 2 * GROUP_W:])
            blk.put("gz", r0, pg[:, 3 * GROUP_W:4 * GROUP_W])
            small = blk.get("small", rows)
            blk.put("gg", r0, -jnp.exp(vec(V_GDN_ALOG)) * _softplus(
                small[:, S_GDN:S_GDN + GROUP_W] + vec(V_GDN_DT)))
            blk.put("gbeta", r0, jax.nn.sigmoid(small[:, S_GDN + GROUP_W:S_GDN + 2 * GROUP_W]))
            yield

        def rwkv_front():
            pr = proj(P_RWKV, P_RWKV_W)
            yield
            rhist[HIST + r0:HIST + r1, :] = pr
            prev = shifted(rhist, 1)
            keep_tail(rhist)
            pr = pr + mu_ref[0:1, :] * (prev - pr)
            rk = pr[:, GROUP_W:2 * GROUP_W]
            rkk = rk * vec(V_RW_KK)
            kk_ss = _head_sum(rkk * rkk, m)
            yield
            up = blk.get("small", rows)[:, S_RW_UP:S_RW_UP + 2 * GROUP_W]
            ra = jax.nn.sigmoid(vec(V_RW_A0) + up[:, GROUP_W:])
            rr = pr[:, :GROUP_W]
            rk = rk * (1.0 + (ra - 1.0) * vec(V_RW_KA))
            rk_sum = _head_sum(rr * rk * vec(V_RW_RK), m)
            rv = pr[:, 2 * GROUP_W:3 * GROUP_W]
            blk.put("wlog", r0, -math.exp(-0.5) * jax.nn.sigmoid(vec(V_RW_W0) + up[:, :GROUP_W]))
            blk.put("rkk", r0, rkk * lax.rsqrt(kk_ss + EPS))
            blk.put("ra", r0, ra)
            blk.put("rr", r0, rr)
            blk.put("rk", r0, rk)
            blk.put("rv", r0, rv)
            blk.put("rz", r0, pr[:, 3 * GROUP_W:4 * GROUP_W])
            yield
            blk.put("bonus", r0, rk_sum * rv)
            yield

        def sc_gla_front():
            ps = proj(P_SC, P_SC_W)
            pa = proj(P_GLA, P_GLA_W)
            yield
            shist[HIST + r0:HIST + r1, :] = (ps[:, GROUP_W:2 * GROUP_W]
                                             * ps[:, 2 * GROUP_W:3 * GROUP_W])
            cv = shifted(shist, SC_CONV - 1) * scw_ref[0:1, :]
            for i in range(1, SC_CONV):
                cv = cv + shifted(shist, SC_CONV - 1 - i) * scw_ref[i:i + 1, :]
            keep_tail(shist)
            ybuf[rows, 2 * GROUP_W:3 * GROUP_W] = ps[:, :GROUP_W] * cv * _silu(ps[:, 3 * GROUP_W:])
            yield
            pre_gate = (blk.get("small", rows)[:, S_GLA:S_GLA + GLA_K]
                        + vec_ref[V_GLA_BIAS:V_GLA_BIAS + 1, :GLA_K])
            blk.put("loga", r0, -_softplus(-pre_gate) * (1.0 / GLA_TAU))
            blk.put("aq", r0, pa[:, :GLA_K])
            blk.put("ak", r0, pa[:, GLA_K:2 * GLA_K])
            blk.put("av", r0, pa[:, 2 * GLA_K:2 * GLA_K + GROUP_W])
            blk.put("az", r0, pa[:, 2 * GLA_K + GROUP_W:2 * GLA_K + 2 * GROUP_W])
            yield

        return [low_front(), gdn_front(), rwkv_front(), sc_gla_front()]

    def finish(group):
        rows = slice(group[0] * CHUNK, (group[-1] + 1) * CHUNK)
        o_g = ybuf[rows, 0:GROUP_W]
        y_r = ybuf[rows, GROUP_W:2 * GROUP_W]
        o_a = ybuf[rows, 3 * GROUP_W:]
        g_ss = _head_sum(o_g * o_g, m)
        r_mean = _head_sum(y_r, m) * (1.0 / HEAD_V)
        a_ss = _head_sum(o_a * o_a, m)
        yield
        yc = y_r - r_mean
        var = _head_sum(yc * yc, m) * (1.0 / HEAD_V)
        o_g = o_g * lax.rsqrt(g_ss * (1.0 / HEAD_V) + EPS) * vec(V_GDN_NW)
        ybuf[rows, 0:GROUP_W] = o_g * _silu(blk.get("gz", rows))
        o_a = o_a * lax.rsqrt(a_ss * (1.0 / HEAD_V) + EPS) * vec(V_GLA_NW)
        ybuf[rows, 3 * GROUP_W:] = o_a * _silu(blk.get("az", rows))
        yield
        yn = yc * lax.rsqrt(var + RWKV_GN_EPS) * vec(V_RW_LNW) + vec(V_RW_LNB)
        ybuf[rows, GROUP_W:2 * GROUP_W] = (yn + blk.get("bonus", rows)) * _silu(blk.get("rz", rows))
        out = jnp.dot(ybuf[rows, :].astype(BF16), wout_ref[...], preferred_element_type=F32)
        yield
        out = out * lax.rsqrt(jnp.mean(out * out, axis=-1, keepdims=True) + EPS) * postw_ref[...]
        o_ref[0, rows, :] = x[rows] + out
        yield

    n_chunks = tb // CHUNK
    groups, start = [], 0
    while start < n_chunks:
        size = GROUP_SIZES[min(len(groups), len(GROUP_SIZES) - 1)]
        groups.append(list(range(start, min(start + size, n_chunks))))
        start += size
    split = min(tb, -(-len(groups[0]) * CHUNK // FRONT_ROWS) * FRONT_ROWS)
    pre = {}
    st = dict(gdn=s_gdn[...], rw=s_rw[...], gla=s_gla[...])
    _round_robin(*fronts(0, split, last=split == tb))
    rest = fronts(split, tb, last=True) if split < tb else []
    _round_robin(_prep_group(blk, groups[0], m, pre), *rest)
    for gi, group in enumerate(groups):
        gens = [_step_group(group, pre, st, ybuf, m)]
        if gi + 1 < len(groups):
            gens.append(_prep_group(blk, groups[gi + 1], m, pre))
        if gi > 0:
            gens.append(finish(groups[gi - 1]))
        _round_robin(*gens)
    _round_robin(finish(groups[-1]))
    s_gdn[...] = st["gdn"]
    s_rw[...] = st["rw"]
    s_gla[...] = st["gla"]


def _layer(x, prew, win, gconv, vec, mu, mulow, wlow, scw, wout, postw, *, tb):
    bsz, seq, _ = x.shape
    const = lambda b, t: (0, 0)
    full = lambda a: pl.BlockSpec(a.shape, const)
    return pl.pallas_call(
        functools.partial(_layer_kernel, tb=tb),
        grid=(bsz, seq // tb),
        in_specs=[pl.BlockSpec((1, tb, D_MODEL), lambda b, t: (b, t, 0))]
        + [full(a) for a in (prew, win, gconv, vec, mu, mulow, wlow, scw, wout, postw)],
        out_specs=pl.BlockSpec((1, tb, D_MODEL), lambda b, t: (b, t, 0)),
        out_shape=jax.ShapeDtypeStruct(x.shape, F32),
        scratch_shapes=[
            pltpu.VMEM((HIST + tb, 3 * GROUP_W), F32),
            pltpu.VMEM((HIST + tb, P_RWKV_W), F32),
            pltpu.VMEM((HIST + tb, P_LOW_W), F32),
            pltpu.VMEM((HIST + tb, GROUP_W), F32),
            pltpu.VMEM((GROUP_W, GROUP_W), F32),
            pltpu.VMEM((GROUP_W, GROUP_W), F32),
            pltpu.VMEM((GROUP_W, GLA_K), F32),
            pltpu.VMEM((tb, 4 * GROUP_W), F32),
        ],
        compiler_params=pltpu.CompilerParams(
            dimension_semantics=("parallel", "arbitrary"),
            vmem_limit_bytes=56 * 1024 * 1024),
        name="hybrid_layer",
    )(x, prew, win, gconv, vec, mu, mulow, wlow, scw, wout, postw)


def _pad_rows(a, rows):
    return jnp.pad(a, ((0, rows - a.shape[0]), (0, 0)))


def _prep_layer(w_in, gdn_conv_w, gdn_a_log, gdn_dt_bias, gdn_norm_w, rwkv_mu, rwkv_w0, rwkv_w_up,
                rwkv_a0, rwkv_a_up, rwkv_k_k, rwkv_k_a, rwkv_r_k, rwkv_ln_w, rwkv_ln_b, sc_conv_w,
                gla_a_up, gla_a_bias, gla_norm_w):
    g0, r0, a0 = _GDN0, _RWKV0, _GLA0
    gla_w = 2 * GLA_K + 2 * GROUP_W
    narrow = [w_in[:, r0 + 4 * GROUP_W:r0 + _RWKV_COLS],
              w_in[:, a0 + gla_w:a0 + gla_w + GLA_RANK],
              w_in[:, g0 + 4 * GROUP_W:g0 + _GDN_COLS]]
    used = L_GDN_BETA + N_HEADS
    win = jnp.concatenate(
        [w_in[:, g0:g0 + 4 * GROUP_W], w_in[:, r0:r0 + 4 * GROUP_W], w_in[:, _SC0:_SC0 + _SC_COLS],
         w_in[:, a0:a0 + gla_w]] + narrow + [jnp.zeros((D_MODEL, P_LOW_W - used), F32)],
        axis=1).astype(BF16)

    def per_head(a):
        return jnp.repeat(a, HEAD_V)

    def per_dim(a):
        return jnp.tile(a, N_HEADS)

    rows = [None] * 12
    rows[V_GDN_ALOG] = per_head(gdn_a_log)
    rows[V_GDN_DT] = per_head(gdn_dt_bias)
    rows[V_GDN_NW] = per_dim(gdn_norm_w)
    rows[V_RW_W0] = rwkv_w0
    rows[V_RW_A0] = rwkv_a0
    rows[V_RW_KK] = rwkv_k_k
    rows[V_RW_KA] = rwkv_k_a
    rows[V_RW_RK] = rwkv_r_k
    rows[V_RW_LNW] = rwkv_ln_w
    rows[V_RW_LNB] = rwkv_ln_b
    rows[V_GLA_NW] = per_dim(gla_norm_w)
    rows[V_GLA_BIAS] = jnp.pad(gla_a_bias, (0, GROUP_W - GLA_K))
    vec = _pad_rows(jnp.stack(rows), N_VEC_ROWS)

    expand = jnp.repeat(jnp.eye(N_HEADS, dtype=F32), HEAD_V, axis=1)
    wlow = jnp.zeros((P_LOW_W, S_TOTAL), F32)
    wlow = wlow.at[L_RW_W:L_RW_W + RWKV_RANK, S_RW_UP:S_RW_UP + GROUP_W].set(rwkv_w_up)
    wlow = wlow.at[L_RW_A:L_RW_A + RWKV_RANK, S_RW_UP + GROUP_W:S_RW_UP + 2 * GROUP_W].set(rwkv_a_up)
    wlow = wlow.at[L_GLA_A:L_GLA_A + GLA_RANK, S_GLA:S_GLA + GLA_K].set(gla_a_up)
    wlow = wlow.at[L_GDN_ALPHA:L_GDN_ALPHA + N_HEADS, S_GDN:S_GDN + GROUP_W].set(expand)
    wlow = wlow.at[L_GDN_BETA:L_GDN_BETA + N_HEADS, S_GDN + GROUP_W:S_GDN + 2 * GROUP_W].set(expand)
    mu_low = jnp.pad(rwkv_mu[4 * GROUP_W:], (0, P_LOW_W - 2 * RWKV_RANK))[None, :]
    return (win, _pad_rows(gdn_conv_w, HIST), vec, rwkv_mu[None, :4 * GROUP_W], mu_low,
            wlow.astype(BF16), _pad_rows(sc_conv_w, HIST))


def _time_block(seq):
    for tb in (512, 256, 128, 64):
        if seq % tb == 0:
            return tb
    raise ValueError("sequence length must be a multiple of the chunk length")


def kernel(x, pre_norm_w, w_in, gdn_conv_w, gdn_a_log, gdn_dt_bias, gdn_norm_w, rwkv_mu, rwkv_w0,
           rwkv_w_up, rwkv_a0, rwkv_a_up, rwkv_k_k, rwkv_k_a, rwkv_r_k, rwkv_ln_w, rwkv_ln_b,
           sc_conv_w, gla_a_up, gla_a_bias, gla_norm_w, w_out, post_norm_w):
    tb = _time_block(x.shape[1])
    for l in range(w_in.shape[0]):
        win, gconv, vec, mu, mulow, wlow, scw = _prep_layer(
            w_in[l], gdn_conv_w[l], gdn_a_log[l], gdn_dt_bias[l], gdn_norm_w[l], rwkv_mu[l],
            rwkv_w0[l], rwkv_w_up[l], rwkv_a0[l], rwkv_a_up[l], rwkv_k_k[l], rwkv_k_a[l],
            rwkv_r_k[l], rwkv_ln_w[l], rwkv_ln_b[l], sc_conv_w[l], gla_a_up[l], gla_a_bias[l],
            gla_norm_w[l])
        x = _layer(x, pre_norm_w[l][None, :], win, gconv, vec, mu, mulow, wlow, scw,
                   w_out[l].astype(BF16), post_norm_w[l][None, :], tb=tb)
    return x
```

```python
import functools
import math

import jax
import jax.numpy as jnp
from jax import lax
from jax.experimental import pallas as pl
from jax.experimental.pallas import tpu as pltpu

F32 = jnp.float32
BF16 = jnp.bfloat16

D_MODEL = 1024
GROUP_W = 256
HEAD_V = 64
N_HEADS = 4
GDN_CONV = 4
RWKV_RANK = 64
RWKV_GN_EPS = 64e-5
SC_CONV = 3
GLA_HEAD_K = 32
GLA_K = 128
GLA_RANK = 16
GLA_TAU = 16.0
CHUNK = 64
EPS = 1e-6
LANE = 128
HIST = 8
GROUP_CHUNKS = 3

_GDN0 = 0
_GDN_COLS = 4 * GROUP_W + 2 * N_HEADS
_RWKV0 = _GDN0 + _GDN_COLS
_RWKV_COLS = 4 * GROUP_W + 2 * RWKV_RANK
_SC0 = _RWKV0 + _RWKV_COLS
_SC_COLS = 4 * GROUP_W
_GLA0 = _SC0 + _SC_COLS

P_GDN = 0
P_GDN_W = 4 * GROUP_W
P_RWKV = P_GDN + P_GDN_W
P_RWKV_W = 4 * GROUP_W
P_SC = P_RWKV + P_RWKV_W
P_SC_W = 4 * GROUP_W
P_GLA = P_SC + P_SC_W
P_GLA_W = 2 * GLA_K + 2 * GROUP_W
P_LOW = P_GLA + P_GLA_W
P_LOW_W = 2 * LANE
P_TOTAL = P_LOW + P_LOW_W
L_RW_W = 0
L_RW_A = L_RW_W + RWKV_RANK
L_GLA_A = L_RW_A + RWKV_RANK
L_GDN_ALPHA = L_GLA_A + GLA_RANK
L_GDN_BETA = L_GDN_ALPHA + N_HEADS
S_RW_UP = 0
S_GLA = S_RW_UP + 2 * GROUP_W
S_GDN = S_GLA + GLA_K
S_TOTAL = S_GDN + 2 * GROUP_W

(V_GDN_ALOG, V_GDN_DT, V_GDN_NW, V_RW_W0, V_RW_A0, V_RW_KK, V_RW_KA, V_RW_RK, V_RW_LNW, V_RW_LNB,
 V_GLA_NW, V_GLA_BIAS) = range(12)
N_VEC_ROWS = 16


def _mm(a, b):
    return jnp.dot(a.astype(BF16), b.astype(BF16), preferred_element_type=F32)


def _mm_nt(a, b):
    return lax.dot_general(a.astype(BF16), b.astype(BF16), (((1,), (1,)), ((), ())),
                           preferred_element_type=F32)


def _mm_tn(a, b):
    return lax.dot_general(a.astype(BF16), b.astype(BF16), (((0,), (0,)), ((), ())),
                           preferred_element_type=F32)


def _split(a):
    hi = a.astype(BF16)
    lo = (a - hi.astype(F32)).astype(BF16)
    return hi, lo


def _xmm(a_exact, b):
    hi, lo = _split(b)
    return (jnp.dot(a_exact, hi, preferred_element_type=F32)
            + jnp.dot(a_exact, lo, preferred_element_type=F32))


def _silu(x):
    return x * jax.nn.sigmoid(x)


def _softplus(x):
    return jnp.maximum(x, 0.0) + jnp.log1p(jnp.exp(-jnp.abs(x)))


def _iota(shape, dim):
    return lax.broadcasted_iota(jnp.int32, shape, dim)


class _Masks:
    def __init__(self):
        row = _iota((CHUNK, GROUP_W), 0)
        col = _iota((CHUNK, GROUP_W), 1) & (CHUNK - 1)
        self.incl = row >= col
        self.strict = row > col
        self.eye = row == col
        r2 = _iota((GROUP_W, GROUP_W), 0)
        c2 = _iota((GROUP_W, GROUP_W), 1)
        self.bd = (r2 // HEAD_V) == (c2 // HEAD_V)
        self.ones_bd = jnp.where(self.bd, 1.0, 0.0).astype(BF16)
        r3 = _iota((GROUP_W, GLA_K), 0)
        c3 = _iota((GROUP_W, GLA_K), 1)
        self.bd_gla = (r3 // HEAD_V) == (c3 // GLA_HEAD_K)
        r4 = _iota((CHUNK, CHUNK), 0)
        c4 = _iota((CHUNK, CHUNK), 1)
        self.tril = jnp.where(r4 >= c4, 1.0, 0.0).astype(BF16)


def _bd(x, mask):
    x = x.astype(BF16)
    return jnp.where(mask, jnp.concatenate([x] * N_HEADS, axis=0), jnp.zeros((), BF16))


def _head_sum(x, m):
    return jnp.dot(x.astype(BF16), m.ones_bd, preferred_element_type=F32)


def _cumsum_chunk(x, m):
    return _xmm(m.tril, x)


def _mm_heads(a, b, m):
    return jnp.dot(a.astype(BF16), _bd(b, m.bd), preferred_element_type=F32)


def _round_robin(*gens):
    live = list(gens)
    while live:
        for g in tuple(live):
            try:
                next(g)
            except StopIteration:
                live.remove(g)


def _inverse_unit_lower(ns, m):
    eye = jnp.where(m.eye, 1.0, 0.0)
    ps = [eye + n for n in ns]
    pws = [_mm_heads(n, n, m) for n in ns]
    yield
    for _ in range(4):
        both = [_mm_heads(jnp.concatenate([pw, p], axis=0), pw, m)
                for pw, p in zip(pws, ps)]
        ps = [p + b[CHUNK:] for p, b in zip(ps, both)]
        pws = [b[:CHUNK] for b in both]
        yield
    return [p + _mm_heads(p, pw, m) for p, pw in zip(ps, pws)]


def _gdn_prep(q, k, v, beta, gc, m):
    g_row = jnp.sum(jnp.where(m.eye, gc, 0.0), axis=0, keepdims=True)
    g_last = gc[CHUNK - 1:CHUNK, :]
    decay = jnp.where(m.incl, jnp.exp(jnp.where(m.incl, gc - g_row, 0.0)), 0.0)
    kb = k * beta
    aq = _mm_nt(jnp.concatenate([kb, q], axis=0), _bd(k, m.bd))
    e_gc = jnp.exp(gc)
    return dict(n=-jnp.where(m.strict, aq[:CHUNK] * decay, 0.0), attn=aq[CHUNK:] * decay,
                vb=v * beta, kbg=kb * e_gc, qg=q * e_gc, k_end=k * jnp.exp(g_last - gc),
                e_last=jnp.exp(g_last))


def _rwkv_prep(r, k, v, kk, a, wlog, cw, m):
    cw_last = cw[CHUNK - 1:CHUNK, :]
    e_ncw = jnp.exp(-cw)
    e_end = jnp.exp(cw_last - cw)
    b = kk * a
    lhs = jnp.concatenate([-kk * jnp.exp(cw - wlog), r * jnp.exp(cw)], axis=0)
    m_b = _mm_nt(lhs, _bd(b * e_ncw, m.bd))
    m_k = _mm_nt(lhs, _bd(k * e_ncw, m.bd))
    a_kv = jnp.concatenate([jnp.where(m.strict, m_k[:CHUNK], 0.0),
                            jnp.where(m.incl, m_k[CHUNK:], 0.0)], axis=0)
    return dict(n=jnp.where(m.strict, m_b[:CHUNK], 0.0), a_rb=jnp.where(m.incl, m_b[CHUNK:], 0.0),
                lhs=lhs, a_kv=a_kv, v=v, ends=jnp.concatenate([b * e_end, k * e_end], axis=0),
                e_last=jnp.exp(cw_last))


def _gla_prep(q, k, v, bc, m):
    b_last = bc[CHUNK - 1:CHUNK, :]
    qe = q * (GLA_HEAD_K ** -0.5) * jnp.exp(bc)
    ke = k * jnp.exp(-bc)
    attn = jnp.where(m.incl, _mm_nt(qe, _bd(ke, m.bd_gla)), 0.0)
    return dict(qe=qe, attn=attn, v=v, k_end=k * jnp.exp(b_last - bc), e_last=jnp.exp(b_last))


def _prep_group(blk, group, m, pre):
    sl = {c: slice(c * CHUNK, (c + 1) * CHUNK) for c in group}
    gc = {c: _cumsum_chunk(blk["gg"][sl[c]], m) for c in group}
    cw = {c: _cumsum_chunk(blk["wlog"][sl[c]], m) for c in group}
    bc = {c: _cumsum_chunk(blk["loga"][sl[c]], m) for c in group}
    yield
    gdn = {c: _gdn_prep(blk["gq"][sl[c]], blk["gk"][sl[c]], blk["gv"][sl[c]], blk["gbeta"][sl[c]],
                        gc[c], m) for c in group}
    rw = {c: _rwkv_prep(blk["rr"][sl[c]], blk["rk"][sl[c]], blk["rv"][sl[c]], blk["rkk"][sl[c]],
                        blk["ra"][sl[c]], blk["wlog"][sl[c]], cw[c], m) for c in group}
    gla = {c: _gla_prep(blk["aq"][sl[c]], blk["ak"][sl[c]], blk["av"][sl[c]], bc[c], m)
           for c in group}
    yield
    for c in group:
        rw[c]["kv"] = _mm(rw[c]["a_kv"], _bd(rw[c]["v"], m.bd))
        gla[c]["intra"] = _mm(gla[c]["attn"], _bd(gla[c]["v"], m.bd))
    t_inv = yield from _inverse_unit_lower([gdn[c]["n"] for c in group]
                                           + [rw[c]["n"] for c in group], m)
    yield
    for i, c in enumerate(group):
        gdn[c]["u"] = _mm_heads(t_inv[i], gdn[c]["vb"], m)
        gdn[c]["w"] = _mm_heads(t_inv[i], gdn[c]["kbg"], m)
        rw[c]["t_inv"] = t_inv[len(group) + i]
        pre[c] = dict(gdn=gdn[c], rw=rw[c], gla=gla[c])
    yield


def _step_group(group, pre, st, ybuf, m):
    for c in group:
        sl = slice(c * CHUNK, (c + 1) * CHUNK)
        g, r, a = pre[c]["gdn"], pre[c]["rw"], pre[c]["gla"]
        wq_s = _mm(jnp.concatenate([g["w"], g["qg"]], axis=0), st["gdn"])
        hs = _mm_nt(r["lhs"], st["rw"])
        ybuf[sl, 3 * GROUP_W:] = _mm_nt(a["qe"], st["gla"]) + a["intra"]
        st["gla"] = (st["gla"] * a["e_last"]
                     + jnp.where(m.bd_gla, _mm_tn(a["v"], a["k_end"]), 0.0))
        yield
        v_new = g["u"] - wq_s[:CHUNK]
        ybuf[sl, 0:GROUP_W] = wq_s[CHUNK:] + _mm(g["attn"], _bd(v_new, m.bd))
        st["gdn"] = st["gdn"] * g["e_last"] + jnp.where(m.bd, _mm_tn(g["k_end"], v_new), 0.0)
        u = _mm_heads(r["t_inv"], hs[:CHUNK] + r["kv"][:CHUNK], m)
        yield
        ybuf[sl, GROUP_W:2 * GROUP_W] = (hs[CHUNK:] + r["kv"][CHUNK:]
                                         + _mm(r["a_rb"], _bd(u, m.bd)))
        upd = _mm_tn(jnp.concatenate([u, r["v"]], axis=0), r["ends"])
        st["rw"] = st["rw"] * r["e_last"] + jnp.where(m.bd, upd, 0.0)
        yield


def _layer_kernel(x_ref, prew_ref, win_ref, gconv_ref, vec_ref, mu_ref, mulow_ref, wlow_ref, scw_ref,
                  wout_ref, postw_ref, o_ref,
                  ghist, rhist, lhist, shist, s_gdn, s_rw, s_gla, ybuf, *, tb):
    t_idx = pl.program_id(1)

    @pl.when(t_idx == 0)
    def _():
        ghist[0:HIST, :] = jnp.zeros((HIST, ghist.shape[1]), F32)
        rhist[0:HIST, :] = jnp.zeros((HIST, rhist.shape[1]), F32)
        lhist[0:HIST, :] = jnp.zeros((HIST, lhist.shape[1]), F32)
        shist[0:HIST, :] = jnp.zeros((HIST, shist.shape[1]), F32)
        s_gdn[...] = jnp.zeros_like(s_gdn)
        s_rw[...] = jnp.zeros_like(s_rw)
        s_gla[...] = jnp.zeros_like(s_gla)

    m = _Masks()

    def vec(row):
        return vec_ref[row:row + 1, :]

    x = x_ref[0]
    h = x * lax.rsqrt(jnp.mean(x * x, axis=-1, keepdims=True) + EPS) * prew_ref[...]
    h = h.astype(BF16)
    blk = {}

    def proj(start, width):
        return jnp.dot(h, win_ref[:, start:start + width], preferred_element_type=F32)

    def low_front():
        low = proj(P_LOW, P_LOW_W)
        yield
        lhist[HIST:HIST + tb, :] = low
        prev = lhist[pl.ds(HIST - 1, tb), :]
        lhist[0:HIST, :] = lhist[tb:tb + HIST, :]
        low = low + mulow_ref[0:1, :] * (prev - low)
        lane = _iota(low.shape, 1)
        low = jnp.where((lane >= L_RW_W) & (lane < L_RW_A), jnp.tanh(low), low)
        blk["small"] = _mm(low, wlow_ref[...])
        yield

    def gdn_front():
        pg = proj(P_GDN, P_GDN_W)
        yield
        ghist[HIST:HIST + tb, :] = pg[:, :3 * GROUP_W]
        qkv = ghist[pl.ds(HIST - GDN_CONV + 1, tb), :] * gconv_ref[0:1, :]
        for i in range(1, GDN_CONV):
            qkv = qkv + ghist[pl.ds(HIST - GDN_CONV + 1 + i, tb), :] * gconv_ref[i:i + 1, :]
        ghist[0:HIST, :] = ghist[tb:tb + HIST, :]
        qkv = _silu(qkv)
        gq = qkv[:, :GROUP_W]
        gk = qkv[:, GROUP_W:2 * GROUP_W]
        q_ss = _head_sum(gq * gq, m)
        k_ss = _head_sum(gk * gk, m)
        yield
        blk["gq"] = gq * lax.rsqrt(q_ss + EPS) * (HEAD_V ** -0.5)
        blk["gk"] = gk * lax.rsqrt(k_ss + EPS)
        blk["gv"] = qkv[:, 2 * GROUP_W:]
        blk["gz"] = pg[:, 3 * GROUP_W:4 * GROUP_W]
        small = blk["small"]
        blk["gg"] = -jnp.exp(vec(V_GDN_ALOG)) * _softplus(
            small[:, S_GDN:S_GDN + GROUP_W] + vec(V_GDN_DT))
        blk["gbeta"] = jax.nn.sigmoid(small[:, S_GDN + GROUP_W:S_GDN + 2 * GROUP_W])
        yield

    def rwkv_front():
        pr = proj(P_RWKV, P_RWKV_W)
        yield
        rhist[HIST:HIST + tb, :] = pr
        prev = rhist[pl.ds(HIST - 1, tb), :]
        rhist[0:HIST, :] = rhist[tb:tb + HIST, :]
        pr = pr + mu_ref[0:1, :] * (prev - pr)
        rk = pr[:, GROUP_W:2 * GROUP_W]
        rkk = rk * vec(V_RW_KK)
        kk_ss = _head_sum(rkk * rkk, m)
        yield
        up = blk["small"][:, S_RW_UP:S_RW_UP + 2 * GROUP_W]
        ra = jax.nn.sigmoid(vec(V_RW_A0) + up[:, GROUP_W:])
        rr = pr[:, :GROUP_W]
        rk = rk * (1.0 + (ra - 1.0) * vec(V_RW_KA))
        rk_sum = _head_sum(rr * rk * vec(V_RW_RK), m)
        blk["wlog"] = -math.exp(-0.5) * jax.nn.sigmoid(vec(V_RW_W0) + up[:, :GROUP_W])
        blk["rkk"] = rkk * lax.rsqrt(kk_ss + EPS)
        blk["ra"], blk["rr"], blk["rk"] = ra, rr, rk
        blk["rv"] = pr[:, 2 * GROUP_W:3 * GROUP_W]
        blk["rz"] = pr[:, 3 * GROUP_W:4 * GROUP_W]
        yield
        blk["bonus"] = rk_sum * blk["rv"]
        yield

    def sc_gla_front():
        ps = proj(P_SC, P_SC_W)
        pa = proj(P_GLA, P_GLA_W)
        yield
        shist[HIST:HIST + tb, :] = ps[:, GROUP_W:2 * GROUP_W] * ps[:, 2 * GROUP_W:3 * GROUP_W]
        cv = shist[pl.ds(HIST - SC_CONV + 1, tb), :] * scw_ref[0:1, :]
        for i in range(1, SC_CONV):
            cv = cv + shist[pl.ds(HIST - SC_CONV + 1 + i, tb), :] * scw_ref[i:i + 1, :]
        shist[0:HIST, :] = shist[tb:tb + HIST, :]
        ybuf[:, 2 * GROUP_W:3 * GROUP_W] = ps[:, :GROUP_W] * cv * _silu(ps[:, 3 * GROUP_W:])
        yield
        pre_gate = (blk["small"][:, S_GLA:S_GLA + GLA_K]
                    + vec_ref[V_GLA_BIAS:V_GLA_BIAS + 1, :GLA_K])
        blk["loga"] = -_softplus(-pre_gate) * (1.0 / GLA_TAU)
        blk["aq"] = pa[:, :GLA_K]
        blk["ak"] = pa[:, GLA_K:2 * GLA_K]
        blk["av"] = pa[:, 2 * GLA_K:2 * GLA_K + GROUP_W]
        blk["az"] = pa[:, 2 * GLA_K + GROUP_W:2 * GLA_K + 2 * GROUP_W]
        yield

    def finish(group):
        rows = slice(group[0] * CHUNK, (group[-1] + 1) * CHUNK)
        o_g = ybuf[rows, 0:GROUP_W]
        y_r = ybuf[rows, GROUP_W:2 * GROUP_W]
        o_a = ybuf[rows, 3 * GROUP_W:]
        g_ss = _head_sum(o_g * o_g, m)
        r_mean = _head_sum(y_r, m) * (1.0 / HEAD_V)
        a_ss = _head_sum(o_a * o_a, m)
        yield
        yc = y_r - r_mean
        var = _head_sum(yc * yc, m) * (1.0 / HEAD_V)
        o_g = o_g * lax.rsqrt(g_ss * (1.0 / HEAD_V) + EPS) * vec(V_GDN_NW)
        ybuf[rows, 0:GROUP_W] = o_g * _silu(blk["gz"][rows])
        o_a = o_a * lax.rsqrt(a_ss * (1.0 / HEAD_V) + EPS) * vec(V_GLA_NW)
        ybuf[rows, 3 * GROUP_W:] = o_a * _silu(blk["az"][rows])
        yield
        yn = yc * lax.rsqrt(var + RWKV_GN_EPS) * vec(V_RW_LNW) + vec(V_RW_LNB)
        ybuf[rows, GROUP_W:2 * GROUP_W] = (yn + blk["bonus"][rows]) * _silu(blk["rz"][rows])
        out = jnp.dot(ybuf[rows, :].astype(BF16), wout_ref[...], preferred_element_type=F32)
        yield
        out = out * lax.rsqrt(jnp.mean(out * out, axis=-1, keepdims=True) + EPS) * postw_ref[...]
        o_ref[0, rows, :] = x[rows] + out
        yield

    _round_robin(low_front(), gdn_front(), rwkv_front(), sc_gla_front())

    n_chunks = tb // CHUNK
    groups = [list(range(i, min(i + GROUP_CHUNKS, n_chunks)))
              for i in range(0, n_chunks, GROUP_CHUNKS)]
    pre = {}
    st = dict(gdn=s_gdn[...], rw=s_rw[...], gla=s_gla[...])
    _round_robin(_prep_group(blk, groups[0], m, pre))
    for gi, group in enumerate(groups):
        gens = [_step_group(group, pre, st, ybuf, m)]
        if gi + 1 < len(groups):
            gens.append(_prep_group(blk, groups[gi + 1], m, pre))
        if gi > 0:
            gens.append(finish(groups[gi - 1]))
        _round_robin(*gens)
    _round_robin(finish(groups[-1]))
    s_gdn[...] = st["gdn"]
    s_rw[...] = st["rw"]
    s_gla[...] = st["gla"]


def _layer(x, l, prew, win, gconv, vec, mu, mulow, wlow, scw, wout, postw, *, tb):
    bsz, seq, _ = x.shape
    full = lambda a: pl.BlockSpec((None,) + a.shape[1:], lambda b, t: (l, 0, 0))
    return pl.pallas_call(
        functools.partial(_layer_kernel, tb=tb),
        grid=(bsz, seq // tb),
        in_specs=[pl.BlockSpec((1, tb, D_MODEL), lambda b, t: (b, t, 0))]
        + [full(a) for a in (prew, win, gconv, vec, mu, mulow, wlow, scw, wout, postw)],
        out_specs=pl.BlockSpec((1, tb, D_MODEL), lambda b, t: (b, t, 0)),
        out_shape=jax.ShapeDtypeStruct(x.shape, F32),
        scratch_shapes=[
            pltpu.VMEM((HIST + tb, 3 * GROUP_W), F32),
            pltpu.VMEM((HIST + tb, P_RWKV_W), F32),
            pltpu.VMEM((HIST + tb, P_LOW_W), F32),
            pltpu.VMEM((HIST + tb, GROUP_W), F32),
            pltpu.VMEM((GROUP_W, GROUP_W), F32),
            pltpu.VMEM((GROUP_W, GROUP_W), F32),
            pltpu.VMEM((GROUP_W, GLA_K), F32),
            pltpu.VMEM((tb, 4 * GROUP_W), F32),
        ],
        compiler_params=pltpu.CompilerParams(
            dimension_semantics=("parallel", "arbitrary"),
            vmem_limit_bytes=56 * 1024 * 1024),
        name="hybrid_layer",
    )(x, prew, win, gconv, vec, mu, mulow, wlow, scw, wout, postw)


def _pad_rows(a, rows):
    return jnp.pad(a, ((0, rows - a.shape[0]), (0, 0)))


def _prep_layer(w_in, gdn_conv_w, gdn_a_log, gdn_dt_bias, gdn_norm_w, rwkv_mu, rwkv_w0, rwkv_w_up,
                rwkv_a0, rwkv_a_up, rwkv_k_k, rwkv_k_a, rwkv_r_k, rwkv_ln_w, rwkv_ln_b, sc_conv_w,
                gla_a_up, gla_a_bias, gla_norm_w):
    g0, r0, a0 = _GDN0, _RWKV0, _GLA0
    gla_w = 2 * GLA_K + 2 * GROUP_W
    narrow = [w_in[:, r0 + 4 * GROUP_W:r0 + _RWKV_COLS],
              w_in[:, a0 + gla_w:a0 + gla_w + GLA_RANK],
              w_in[:, g0 + 4 * GROUP_W:g0 + _GDN_COLS]]
    used = L_GDN_BETA + N_HEADS
    win = jnp.concatenate(
        [w_in[:, g0:g0 + 4 * GROUP_W], w_in[:, r0:r0 + 4 * GROUP_W], w_in[:, _SC0:_SC0 + _SC_COLS],
         w_in[:, a0:a0 + gla_w]] + narrow + [jnp.zeros((D_MODEL, P_LOW_W - used), F32)],
        axis=1).astype(BF16)

    def per_head(a):
        return jnp.repeat(a, HEAD_V)

    def per_dim(a):
        return jnp.tile(a, N_HEADS)

    rows = [None] * 12
    rows[V_GDN_ALOG] = per_head(gdn_a_log)
    rows[V_GDN_DT] = per_head(gdn_dt_bias)
    rows[V_GDN_NW] = per_dim(gdn_norm_w)
    rows[V_RW_W0] = rwkv_w0
    rows[V_RW_A0] = rwkv_a0
    rows[V_RW_KK] = rwkv_k_k
    rows[V_RW_KA] = rwkv_k_a
    rows[V_RW_RK] = rwkv_r_k
    rows[V_RW_LNW] = rwkv_ln_w
    rows[V_RW_LNB] = rwkv_ln_b
    rows[V_GLA_NW] = per_dim(gla_norm_w)
    rows[V_GLA_BIAS] = jnp.pad(gla_a_bias, (0, GROUP_W - GLA_K))
    vec = _pad_rows(jnp.stack(rows), N_VEC_ROWS)

    expand = jnp.repeat(jnp.eye(N_HEADS, dtype=F32), HEAD_V, axis=1)
    wlow = jnp.zeros((P_LOW_W, S_TOTAL), F32)
    wlow = wlow.at[L_RW_W:L_RW_W + RWKV_RANK, S_RW_UP:S_RW_UP + GROUP_W].set(rwkv_w_up)
    wlow = wlow.at[L_RW_A:L_RW_A + RWKV_RANK, S_RW_UP + GROUP_W:S_RW_UP + 2 * GROUP_W].set(rwkv_a_up)
    wlow = wlow.at[L_GLA_A:L_GLA_A + GLA_RANK, S_GLA:S_GLA + GLA_K].set(gla_a_up)
    wlow = wlow.at[L_GDN_ALPHA:L_GDN_ALPHA + N_HEADS, S_GDN:S_GDN + GROUP_W].set(expand)
    wlow = wlow.at[L_GDN_BETA:L_GDN_BETA + N_HEADS, S_GDN + GROUP_W:S_GDN + 2 * GROUP_W].set(expand)
    mu_low = jnp.pad(rwkv_mu[4 * GROUP_W:], (0, P_LOW_W - 2 * RWKV_RANK))[None, :]
    return (win, _pad_rows(gdn_conv_w, HIST), vec, rwkv_mu[None, :4 * GROUP_W], mu_low,
            wlow.astype(BF16), _pad_rows(sc_conv_w, HIST))


def _time_block(seq):
    for tb in (512, 256, 128, 64):
        if seq % tb == 0:
            return tb
    raise ValueError("sequence length must be a multiple of the chunk length")


def kernel(x, pre_norm_w, w_in, gdn_conv_w, gdn_a_log, gdn_dt_bias, gdn_norm_w, rwkv_mu, rwkv_w0,
           rwkv_w_up, rwkv_a0, rwkv_a_up, rwkv_k_k, rwkv_k_a, rwkv_r_k, rwkv_ln_w, rwkv_ln_b,
           sc_conv_w, gla_a_up, gla_a_bias, gla_norm_w, w_out, post_norm_w):
    tb = _time_block(x.shape[1])
    win, gconv, vec, mu, mulow, wlow, scw = jax.vmap(_prep_layer)(
        w_in, gdn_conv_w, gdn_a_log, gdn_dt_bias, gdn_norm_w, rwkv_mu, rwkv_w0, rwkv_w_up, rwkv_a0,
        rwkv_a_up, rwkv_k_k, rwkv_k_a, rwkv_r_k, rwkv_ln_w, rwkv_ln_b, sc_conv_w, gla_a_up,
        gla_a_bias, gla_norm_w)
    prew, postw, wout = pre_norm_w[:, None, :], post_norm_w[:, None, :], w_out.astype(BF16)
    for l in range(w_in.shape[0]):
        x = _layer(x, l, prew, win, gconv, vec, mu, mulow, wlow, scw, wout, postw, tb=tb)
    return x
```
